```python
import math
import jax, jax.numpy as jnp
from jax import lax
import numpy as np

D_MODEL = 1024
BATCH = 16
SEQ = 4096
DEPTH = 1

D_MIX = D_MODEL
D_SSM = D_MIX // 2
D_FNO = D_MIX - D_SSM
SSM_GROUP = 16
N_SSM_GROUPS = D_SSM // SSM_GROUP
SSM_STATE = 64
N_FOURIER_GROUPS = 4
FOURIER_GROUP = D_FNO // N_FOURIER_GROUPS
DT_MIN = 1e-3
DT_MAX = 1e-1
PEER_HEADS = 8
PEER_NKEYS = 128
PEER_EXPERTS = PEER_NKEYS * PEER_NKEYS
PEER_TOPK = 16
PEER_QDIM = 256
PEER_HALF = PEER_QDIM // 2
PEER_BLOCK = 128
RMS_EPS = 1e-6

kernel_name = "hybrid_s5_fnet_peer_encoder_block"


def rmsnorm(x, g):
    xf = x.astype(jnp.float32)
    y = xf * lax.rsqrt(jnp.mean(xf * xf, axis=-1, keepdims=True) + RMS_EPS)
    return (y * g.astype(jnp.float32)).astype(x.dtype)


def _ssm_combine(e_i, e_j):
    a_i, b_i = e_i
    a_j, b_j = e_j
    return a_j * a_i, a_j * b_i + b_j


def s5_scan(u, a_re, a_im, log_step, b_re, b_im, c_re, c_im, reverse):
    f32 = jnp.float32
    lam = lax.complex(a_re.astype(f32), a_im.astype(f32))
    step = jnp.exp(log_step.astype(f32))[:, None]
    lam_bar = jnp.exp(lam * step)
    b = lax.complex(b_re.astype(f32), b_im.astype(f32))
    b_bar = ((lam_bar - 1.0) / lam)[..., None] * b
    c = lax.complex(c_re.astype(f32), c_im.astype(f32))
    bu = jnp.einsum("gpc,bsgc->bsgp", b_bar, u.astype(jnp.complex64))
    a = jnp.broadcast_to(lam_bar, (1, u.shape[1]) + lam_bar.shape)
    _, states = lax.associative_scan(_ssm_combine, (a, bu), axis=1, reverse=reverse)
    return jnp.einsum("gcp,bsgp->bsgc", c, states).real


def token_mixer(h, w_in, ssm_a_re, ssm_a_im, ssm_log_step, ssm_b_re, ssm_b_im,
                ssm_c_re, ssm_c_im, ssm_d, w_glu, w_fourier, w_out):
    bsz, seq, _ = h.shape
    f32 = jnp.float32
    z = h @ w_in
    u = z[..., :D_SSM].astype(f32).reshape(bsz, seq, N_SSM_GROUPS, SSM_GROUP)
    y = (s5_scan(u, ssm_a_re[0], ssm_a_im[0], ssm_log_step[0], ssm_b_re[0], ssm_b_im[0],
                 ssm_c_re[0], ssm_c_im[0], reverse=False)
         + s5_scan(u, ssm_a_re[1], ssm_a_im[1], ssm_log_step[1], ssm_b_re[1], ssm_b_im[1],
                   ssm_c_re[1], ssm_c_im[1], reverse=True)
         + ssm_d.astype(f32) * u)
    y = jax.nn.gelu(y.reshape(bsz, seq, D_SSM), approximate=False)
    y_ssm = (y * jax.nn.sigmoid(y @ w_glu.astype(f32))).astype(h.dtype)
    f = z[..., D_SSM:].astype(f32).reshape(bsz, seq, N_FOURIER_GROUPS, FOURIER_GROUP)
    f = jnp.fft.fft2(f, axes=(1, 3), norm="ortho").real.astype(h.dtype)
    y_fno = jnp.einsum("bsgc,gcd->bsgd", f, w_fourier).reshape(bsz, seq, D_FNO)
    return jnp.concatenate([y_ssm, y_fno], axis=-1) @ w_out


def peer_ffn(h, w_query, sub_keys, expert_u, expert_v):
    bsz, seq, d = h.shape
    tokens = h.reshape(-1, PEER_BLOCK, d)

    def block(xb):
        t = xb.shape[0]
        q = (xb @ w_query).reshape(t, PEER_HEADS, 2, PEER_HALF)
        scores = jnp.einsum("thpd,hpkd->thpk", q, sub_keys).astype(jnp.float32)
        s_val, s_idx = lax.top_k(scores, PEER_TOPK)
        cand = s_val[:, :, 0, :, None] + s_val[:, :, 1, None, :]
        cand_idx = s_idx[:, :, 0, :, None] * PEER_NKEYS + s_idx[:, :, 1, None, :]
        n_cand = PEER_TOPK * PEER_TOPK
        top_val, top_pos = lax.top_k(cand.reshape(t, PEER_HEADS, n_cand), PEER_TOPK)
        expert_idx = jnp.take_along_axis(cand_idx.reshape(t, PEER_HEADS, n_cand), top_pos, axis=-1)
        gate = jax.nn.softmax(top_val, axis=-1)
        u_sel = expert_u[expert_idx]
        v_sel = expert_v[expert_idx]
        pre = jnp.einsum("thkd,td->thk", u_sel, xb).astype(jnp.float32)
        act = (jax.nn.gelu(pre, approximate=False) * gate).astype(xb.dtype)
        return jnp.einsum("thk,thkd->td", act, v_sel)

    return lax.map(block, tokens).reshape(bsz, seq, d)


def setup_inputs(seed: int = 0) -> dict:
    key = jax.random.key(seed)
    ks = jax.random.split(key, 20)
    nrm = jax.random.normal
    G, P, Cg = N_SSM_GROUPS, SSM_STATE, SSM_GROUP
    n_idx = jnp.arange(P, dtype=jnp.float32)
    x = nrm(ks[0], (BATCH, SEQ, D_MODEL), jnp.float32)
    norm1_g = 1.0 + 0.02 * nrm(ks[1], (DEPTH, D_MODEL), jnp.float32)
    w_in = nrm(ks[2], (DEPTH, D_MODEL, D_MIX), jnp.float32) * D_MODEL ** -0.5
    ssm_a_re = -0.5 + 0.01 * nrm(ks[3], (DEPTH, 2, G, P), jnp.float32)
    ssm_a_im = math.pi * n_idx + 0.01 * nrm(ks[4], (DEPTH, 2, G, P), jnp.float32)
    ssm_log_step = jax.random.uniform(ks[5], (DEPTH, 2, G), jnp.float32,
                                      math.log(DT_MIN), math.log(DT_MAX))
    b_scale = (2.0 * Cg) ** -0.5
    ssm_b_re = nrm(ks[6], (DEPTH, 2, G, P, Cg), jnp.float32) * b_scale
    ssm_b_im = nrm(ks[7], (DEPTH, 2, G, P, Cg), jnp.float32) * b_scale
    c_scale = P ** -0.5
    ssm_c_re = nrm(ks[8], (DEPTH, 2, G, Cg, P), jnp.float32) * c_scale
    ssm_c_im = nrm(ks[9], (DEPTH, 2, G, Cg, P), jnp.float32) * c_scale
    ssm_d = nrm(ks[10], (DEPTH, G, Cg), jnp.float32)
    w_glu = nrm(ks[11], (DEPTH, D_SSM, D_SSM), jnp.float32) * D_SSM ** -0.5
    w_fourier = nrm(ks[12], (DEPTH, N_FOURIER_GROUPS, FOURIER_GROUP, FOURIER_GROUP),
                    jnp.float32) * FOURIER_GROUP ** -0.5
    w_out = nrm(ks[13], (DEPTH, D_MIX, D_MODEL), jnp.float32) * D_MIX ** -0.5
    norm2_g = 1.0 + 0.02 * nrm(ks[14], (DEPTH, D_MODEL), jnp.float32)
    w_query = nrm(ks[15], (DEPTH, D_MODEL, PEER_HEADS * PEER_QDIM), jnp.float32) * D_MODEL ** -0.5
    sub_keys = nrm(ks[16], (DEPTH, PEER_HEADS, 2, PEER_NKEYS, PEER_HALF), jnp.float32) * PEER_HALF ** -0.5
    expert_u = nrm(ks[17], (DEPTH, PEER_EXPERTS, D_MODEL), jnp.float32) * D_MODEL ** -0.5
    expert_v = nrm(ks[18], (DEPTH, PEER_EXPERTS, D_MODEL), jnp.float32) * PEER_HEADS ** -0.5
    final_g = 1.0 + 0.02 * nrm(ks[19], (D_MODEL,), jnp.float32)
    return {"x": x, "norm1_g": norm1_g, "w_in": w_in, "ssm_a_re": ssm_a_re,
            "ssm_a_im": ssm_a_im, "ssm_log_step": ssm_log_step, "ssm_b_re": ssm_b_re,
            "ssm_b_im": ssm_b_im, "ssm_c_re": ssm_c_re, "ssm_c_im": ssm_c_im,
            "ssm_d": ssm_d, "w_glu": w_glu, "w_fourier": w_fourier, "w_out": w_out,
            "norm2_g": norm2_g, "w_query": w_query, "sub_keys": sub_keys,
            "expert_u": expert_u, "expert_v": expert_v, "final_g": final_g}


def reference(x, norm1_g, w_in, ssm_a_re, ssm_a_im, ssm_log_step, ssm_b_re, ssm_b_im,
              ssm_c_re, ssm_c_im, ssm_d, w_glu, w_fourier, w_out, norm2_g, w_query,
              sub_keys, expert_u, expert_v, final_g):
    for layer in range(DEPTH):
        h = rmsnorm(x, norm1_g[layer])
        x = x + token_mixer(h, w_in[layer], ssm_a_re[layer], ssm_a_im[layer],
                            ssm_log_step[layer], ssm_b_re[layer], ssm_b_im[layer],
                            ssm_c_re[layer], ssm_c_im[layer], ssm_d[layer], w_glu[layer],
                            w_fourier[layer], w_out[layer])
        h = rmsnorm(x, norm2_g[layer])
        x = x + peer_ffn(h, w_query[layer], sub_keys[layer], expert_u[layer], expert_v[layer])
    return rmsnorm(x, final_g)
```

```python
import functools
import math

import jax
import jax.numpy as jnp
from jax import lax
from jax.experimental import pallas as pl
from jax.experimental.pallas import tpu as pltpu

F32 = jnp.float32
BF16 = jnp.bfloat16

D_MODEL = 1024
D_SSM = 512
D_FNO = 512
SSM_GROUP = 16
N_SSM_GROUPS = 32
SSM_STATE = 64
N_FOURIER_GROUPS = 4
FOURIER_GROUP = 128
PEER_HEADS = 8
PEER_NKEYS = 128
PEER_EXPERTS = PEER_NKEYS * PEER_NKEYS
PEER_TOPK = 16
PEER_HALF = 128
RMS_EPS = 1e-6

CHUNK = 16
CHUNK_W = CHUNK * SSM_GROUP
LANES = 128
VMEM_LIMIT = 56 * 1024 * 1024

TM_IN = 512
TK_DFT = 512
TT_PEER = 512
SUBLANES = 8
TE_PEER = SUBLANES * PEER_NKEYS

_HI = lax.Precision.HIGHEST


def _rms(x, g):
    return x * lax.rsqrt(jnp.mean(x * x, axis=-1, keepdims=True) + RMS_EPS) * g


def _gelu(x):
    return 0.5 * x * (1.0 + lax.erf(x * math.sqrt(0.5)))


def _inproj_kernel(x_ref, g_ref, w_ref, u_ref, p_ref, q_ref):
    h = _rms(x_ref[...], g_ref[...])
    z = jnp.dot(h.astype(BF16), w_ref[...], preferred_element_type=F32)
    u_ref[...] = z[:, :D_SSM].astype(BF16)
    p_ref[...] = z[:, D_SSM:D_SSM + D_FNO].astype(BF16)
    q_ref[...] = z[:, D_SSM + D_FNO:].astype(BF16)


def _inproj(x, g, w_all):
    bsz, seq, d = x.shape
    nw = w_all.shape[1]
    return pl.pallas_call(
        _inproj_kernel,
        grid=(bsz, seq // TM_IN),
        in_specs=[
            pl.BlockSpec((None, TM_IN, d), lambda b, i: (b, i, 0)),
            pl.BlockSpec((1, d), lambda b, i: (0, 0)),
            pl.BlockSpec((d, nw), lambda b, i: (0, 0)),
        ],
        out_specs=[
            pl.BlockSpec((None, TM_IN, D_SSM), lambda b, i: (b, i, 0)),
            pl.BlockSpec((TM_IN, D_FNO), lambda b, i: (i, b)),
            pl.BlockSpec((TM_IN, D_FNO), lambda b, i: (i, b)),
        ],
        out_shape=[
            jax.ShapeDtypeStruct((bsz, seq, D_SSM), BF16),
            jax.ShapeDtypeStruct((seq, bsz * D_FNO), BF16),
            jax.ShapeDtypeStruct((seq, bsz * D_FNO), BF16),
        ],
        compiler_params=pltpu.CompilerParams(
            dimension_semantics=("parallel", "parallel"), vmem_limit_bytes=VMEM_LIMIT),
        name="inproj",
    )(x, g, w_all)


def _s5_kernel(x_ref, wb_ref, t_ref, wca_ref, wcb_ref, lam_ref, y_ref, s_scr, ha_scr, hb_scr,
               *, n_chunks, bsz):
    rows = n_chunks * bsz
    rblk = min(rows, 512)
    def summarize(r, c):
        rs = pl.ds(pl.multiple_of(r * rblk, rblk), rblk)
        s_scr[rs, :] = jnp.dot(x_ref[rs, :], wb_ref[...], preferred_element_type=F32)
        return c
    lax.fori_loop(0, rows // rblk, summarize, 0)

    lr = lam_ref[0:1, :]
    li = lam_ref[1:2, :]
    is_fwd = lax.broadcasted_iota(jnp.int32, (bsz, LANES), 1) < SSM_STATE

    def step(k, carry):
        hre, him = carry
        rf = pl.ds(pl.multiple_of(k * bsz, bsz), bsz)
        rb = pl.ds(pl.multiple_of((n_chunks - 1 - k) * bsz, bsz), bsz)
        ha_scr[rf, 0:LANES] = hre
        ha_scr[rf, LANES:2 * LANES] = him
        hb_scr[rb, 0:LANES] = hre
        hb_scr[rb, LANES:2 * LANES] = him
        sre = jnp.where(is_fwd, s_scr[rf, 0:LANES], s_scr[rb, 0:LANES])
        sim = jnp.where(is_fwd, s_scr[rf, LANES:2 * LANES], s_scr[rb, LANES:2 * LANES])
        nre = lr * hre - li * him + sre
        nim = lr * him + li * hre + sim
        return nre, nim
    zero = jnp.zeros((bsz, LANES), F32)
    lax.fori_loop(0, n_chunks, step, (zero, zero), unroll=4)

    def emit(r, c):
        rs = pl.ds(pl.multiple_of(r * rblk, rblk), rblk)
        y = jnp.dot(x_ref[rs, :], t_ref[...], preferred_element_type=F32)
        y = y + jnp.dot(ha_scr[rs, :].astype(BF16), wca_ref[...], preferred_element_type=F32)
        y = y + jnp.dot(hb_scr[rs, :].astype(BF16), wcb_ref[...], preferred_element_type=F32)
        y_ref[rs, :] = y.astype(BF16)
        return c
    lax.fori_loop(0, rows // rblk, emit, 0)


def _s5(xg, wb, tt, wca, wcb, lam16, n_chunks, bsz):
    ng, rows, w = xg.shape
    wspec = pl.BlockSpec((None, w, w), lambda g: (g, 0, 0))
    return pl.pallas_call(
        functools.partial(_s5_kernel, n_chunks=n_chunks, bsz=bsz),
        grid=(ng,),
        in_specs=[
            pl.BlockSpec((None, rows, w), lambda g: (g, 0, 0)),
            wspec, wspec, wspec, wspec,
            pl.BlockSpec((None, 2, LANES), lambda g: (g, 0, 0)),
        ],
        out_specs=pl.BlockSpec((None, rows, w), lambda g: (g, 0, 0)),
        out_shape=jax.ShapeDtypeStruct((ng, rows, w), BF16),
        scratch_shapes=[
            pltpu.VMEM((rows, w), F32),
            pltpu.VMEM((rows, w), F32),
            pltpu.VMEM((rows, w), F32),
        ],
        compiler_params=pltpu.CompilerParams(
            dimension_semantics=("parallel",), vmem_limit_bytes=VMEM_LIMIT),
        name="s5",
    )(xg, wb, tt, wca, wcb, lam16)


def _s5_weights(a_re, a_im, log_step, b_re, b_im, c_re, c_im, d_skip):
    g_n, p_n, cg = N_SSM_GROUPS, SSM_STATE, SSM_GROUP
    tau = jnp.arange(CHUNK + 1, dtype=F32)

    def direction(d):
        lam = lax.complex(a_re[d].astype(F32), a_im[d].astype(F32))
        step = jnp.exp(log_step[d].astype(F32))[:, None]
        lam_step = lam * step
        lam_bar = jnp.exp(lam_step)
        b_bar = ((lam_bar - 1.0) / lam)[..., None] * lax.complex(b_re[d].astype(F32), b_im[d].astype(F32))
        c = lax.complex(c_re[d].astype(F32), c_im[d].astype(F32))
        lam_pow = jnp.exp(lam_step[None] * tau[:, None, None])
        kern = jnp.einsum("gop,tgp,gpi->tgoi", c, lam_pow[:CHUNK], b_bar, precision=_HI).real
        return lam_pow, b_bar, c, kern

    lp_f, bb_f, c_f, k_f = direction(0)
    lp_b, bb_b, c_b, k_b = direction(1)

    t_idx = jnp.arange(CHUNK)
    lag = t_idx[None, :] - t_idx[:, None]
    kf = jnp.where((lag >= 0)[:, :, None, None, None], k_f[jnp.clip(lag, 0, CHUNK - 1)], 0.0)
    kb = jnp.where((lag <= 0)[:, :, None, None, None], k_b[jnp.clip(-lag, 0, CHUNK - 1)], 0.0)
    eye_t = jnp.eye(CHUNK, dtype=F32)[:, :, None, None, None]
    eye_c = jnp.eye(cg, dtype=F32)[None, None, None]
    dmat = eye_t * eye_c * d_skip.astype(F32)[None, None, :, :, None]
    tot = kf + kb + dmat
    t_mat = tot.transpose(2, 0, 4, 1, 3).reshape(g_n, CHUNK_W, CHUNK_W)

    wb_f = jnp.einsum("tgp,gpi->gtip", lp_f[:CHUNK][::-1], bb_f).reshape(g_n, CHUNK_W, p_n)
    wb_b = jnp.einsum("tgp,gpi->gtip", lp_b[:CHUNK], bb_b).reshape(g_n, CHUNK_W, p_n)
    wb = jnp.concatenate([wb_f.real, wb_b.real, wb_f.imag, wb_b.imag], axis=-1)

    wc_f = jnp.einsum("gop,tgp->gpto", c_f, lp_f[1:CHUNK + 1]).reshape(g_n, p_n, CHUNK_W)
    wc_b = jnp.einsum("gop,tgp->gpto", c_b, lp_b[1:CHUNK + 1][::-1]).reshape(g_n, p_n, CHUNK_W)
    zeros = jnp.zeros_like(wc_f.real)
    wca = jnp.concatenate([wc_f.real, zeros, -wc_f.imag, zeros], axis=1)
    wcb = jnp.concatenate([zeros, wc_b.real, zeros, -wc_b.imag], axis=1)

    lam16 = jnp.stack([
        jnp.concatenate([lp_f[CHUNK].real, lp_b[CHUNK].real], axis=-1),
        jnp.concatenate([lp_f[CHUNK].imag, lp_b[CHUNK].imag], axis=-1)], axis=1)
    return wb.astype(BF16), t_mat.astype(BF16), wca.astype(BF16), wcb.astype(BF16), lam16


def _dft_kernel(c_ref, s_ref, p_ref, q_ref, o_ref):
    acc = jnp.dot(c_ref[...], p_ref[...], preferred_element_type=F32)
    acc = acc + jnp.dot(s_ref[...], q_ref[...], preferred_element_type=F32)
    o_ref[...] = acc.astype(BF16)


def _seq_dft(cos_t, nsin_t, p, q, bsz):
    seq = cos_t.shape[0]
    return pl.pallas_call(
        _dft_kernel,
        grid=(seq // TK_DFT, bsz),
        in_specs=[
            pl.BlockSpec((TK_DFT, seq), lambda i, b: (i, 0)),
            pl.BlockSpec((TK_DFT, seq), lambda i, b: (i, 0)),
            pl.BlockSpec((seq, D_FNO), lambda i, b: (0, b)),
            pl.BlockSpec((seq, D_FNO), lambda i, b: (0, b)),
        ],
        out_specs=pl.BlockSpec((None, TK_DFT, D_FNO), lambda i, b: (b, i, 0)),
        out_shape=jax.ShapeDtypeStruct((bsz, seq, D_FNO), BF16),
        compiler_params=pltpu.CompilerParams(
            dimension_semantics=("parallel", "arbitrary"), vmem_limit_bytes=VMEM_LIMIT),
        name="seq_dft",
    )(cos_t, nsin_t, p, q)


def _merge_kernel(x_ref, y_ref, f_ref, wglu_ref, wout_ref, g2_ref, wsc_ref,
                  x1_ref, h2_ref, sc_ref):
    yg = _gelu(y_ref[...].astype(F32))
    gate = jax.nn.sigmoid(jnp.dot(yg.astype(BF16), wglu_ref[...], preferred_element_type=F32))
    ys = (yg * gate).astype(BF16)
    mix = jnp.dot(ys, wout_ref[0:D_SSM, :], preferred_element_type=F32)
    mix = mix + jnp.dot(f_ref[...], wout_ref[D_SSM:, :], preferred_element_type=F32)
    x1 = x_ref[...] + mix
    x1_ref[...] = x1
    h2 = _rms(x1, g2_ref[...]).astype(BF16)
    h2_ref[...] = h2
    sc_ref[...] = lax.dot_general(wsc_ref[...], h2, (((1,), (1,)), ((), ())),
                                  preferred_element_type=F32)


def _merge(x2, y, f, wglu, wout, g2, wsc_t):
    n, d = x2.shape
    nsc = wsc_t.shape[0]
    tm = TM_IN
    return pl.pallas_call(
        _merge_kernel,
        grid=(n // tm,),
        in_specs=[
            pl.BlockSpec((tm, d), lambda i: (i, 0)),
            pl.BlockSpec((tm, D_SSM), lambda i: (i, 0)),
            pl.BlockSpec((tm, D_FNO), lambda i: (i, 0)),
            pl.BlockSpec((D_SSM, D_SSM), lambda i: (0, 0)),
            pl.BlockSpec((d, d), lambda i: (0, 0)),
            pl.BlockSpec((1, d), lambda i: (0, 0)),
            pl.BlockSpec((nsc, d), lambda i: (0, 0)),
        ],
        out_specs=[
            pl.BlockSpec((tm, d), lambda i: (i, 0)),
            pl.BlockSpec((tm, d), lambda i: (i, 0)),
            pl.BlockSpec((nsc, tm), lambda i: (0, i)),
        ],
        out_shape=[
            jax.ShapeDtypeStruct((n, d), F32),
            jax.ShapeDtypeStruct((n, d), BF16),
            jax.ShapeDtypeStruct((nsc, n), F32),
        ],
        compiler_params=pltpu.CompilerParams(
            dimension_semantics=("parallel",), vmem_limit_bytes=VMEM_LIMIT),
        name="merge",
    )(x2, y, f, wglu, wout, g2, wsc_t)


_PARTIAL_ROWS = ((2, 5), (3, 4), (4, 3), (5, 2), (6, 2), (7, 2))


def _topk_kernel(sc_ref, st_ref, v_scr):
    neg = -jnp.inf
    riota = lax.broadcasted_iota(jnp.int32, (8, LANES), 0)

    def per_head(h, c):
        for p in range(2):
            s = sc_ref[pl.ds(pl.multiple_of((2 * h + p) * PEER_NKEYS, PEER_NKEYS), PEER_NKEYS), :]
            for k in range(PEER_TOPK):
                m = jnp.max(s, axis=0, keepdims=True)
                v_scr[p, k:k + 1, :] = m
                s = jnp.where(s == m, neg, s)
        v1a = v_scr[0, 0:8, :]
        v1b = v_scr[0, 8:16, :]
        v2a = v_scr[1, 0:8, :]
        v2b = v_scr[1, 8:16, :]
        top1 = v_scr[0, 0:1, :]
        top2 = v_scr[1, 0:1, :]
        blocks = [top1 + v2a, top1 + v2b, v_scr[0, 1:2, :] + v2a]
        for a, nb in _PARTIAL_ROWS:
            blocks.append(jnp.where(riota < nb, v_scr[0, a:a + 1, :] + v2a, neg))
        blocks.append(v1b + top2)
        cand = jnp.concatenate(blocks, axis=0)
        mtop = top1 + top2
        z = jnp.zeros((1, LANES), F32)
        tau = mtop
        for k in range(PEER_TOPK):
            m = jnp.max(cand, axis=0, keepdims=True)
            live = m > neg
            z = z + jnp.where(live, jnp.exp(m - mtop), 0.0)
            tau = jnp.where(live, m, tau)
            cand = jnp.where(cand == m, neg, cand)
        st_ref[pl.ds(h, 1), :] = tau
        st_ref[pl.ds(PEER_HEADS + h, 1), :] = top1
        st_ref[pl.ds(2 * PEER_HEADS + h, 1), :] = top2
        st_ref[pl.ds(3 * PEER_HEADS + h, 1), :] = 1.0 / z
        return c
    lax.fori_loop(0, PEER_HEADS, per_head, 0)


def _topk_stats(sc_t):
    nsc, n = sc_t.shape
    return pl.pallas_call(
        _topk_kernel,
        grid=(n // LANES,),
        in_specs=[pl.BlockSpec((nsc, LANES), lambda i: (0, i))],
        out_specs=pl.BlockSpec((4 * PEER_HEADS, LANES), lambda i: (0, i)),
        out_shape=jax.ShapeDtypeStruct((4 * PEER_HEADS, n), F32),
        scratch_shapes=[pltpu.VMEM((2, PEER_TOPK, LANES), F32)],
        compiler_params=pltpu.CompilerParams(
            dimension_semantics=("parallel",), vmem_limit_bytes=VMEM_LIMIT),
        name="topk_stats",
    )(sc_t)


def _peer_kernel(h2_ref, sc_ref, st_ref, x1_ref, u_ref, vt_ref, gf_ref, o_ref,
                 s1_scr, s2_scr, al_scr, be_scr, pre_scr, a_scr, acc_scr):
    e = pl.program_id(1)
    n_e = pl.num_programs(1)
    rows_per_tile = TE_PEER // PEER_NKEYS
    n_lane_chunks = TT_PEER // LANES

    @pl.when(e == 0)
    def _prologue():
        for h in range(PEER_HEADS):
            s1 = sc_ref[(2 * h) * PEER_NKEYS:(2 * h + 1) * PEER_NKEYS, :]
            s2 = sc_ref[(2 * h + 1) * PEER_NKEYS:(2 * h + 2) * PEER_NKEYS, :]
            m1 = st_ref[PEER_HEADS + h:PEER_HEADS + h + 1, :]
            m2 = st_ref[2 * PEER_HEADS + h:2 * PEER_HEADS + h + 1, :]
            rz = st_ref[3 * PEER_HEADS + h:3 * PEER_HEADS + h + 1, :]
            s1_scr[h] = s1
            s2_scr[h] = s2
            al_scr[h] = jnp.exp(s1 - m1) * rz
            be_scr[h] = jnp.exp(s2 - m2)
        acc_scr[...] = jnp.zeros_like(acc_scr)

    pre_scr[...] = lax.dot_general(u_ref[...], h2_ref[...], (((1,), (1,)), ((), ())),
                                   preferred_element_type=F32)

    i_rows = pl.ds(pl.multiple_of(e * rows_per_tile, rows_per_tile), rows_per_tile)
    for lc in range(n_lane_chunks):
        ls = slice(lc * LANES, (lc + 1) * LANES)
        s1_blk = [s1_scr[h, i_rows, ls] for h in range(PEER_HEADS)]
        al_blk = [al_scr[h, i_rows, ls] for h in range(PEER_HEADS)]
        for il in range(rows_per_tile):
            rs = slice(il * PEER_NKEYS, (il + 1) * PEER_NKEYS)
            g = jnp.zeros((PEER_NKEYS, LANES), F32)
            for h in range(PEER_HEADS):
                keep = (s1_blk[h][il:il + 1, :] + s2_scr[h, :, ls]) >= st_ref[h:h + 1, ls]
                g = g + jnp.where(keep, be_scr[h, :, ls] * al_blk[h][il:il + 1, :], 0.0)
            act = _gelu(pre_scr[rs, ls]) * g
            a_scr[rs, ls] = act.astype(BF16)

    acc_scr[...] += jnp.dot(vt_ref[...], a_scr[...], preferred_element_type=F32)

    @pl.when(e == n_e - 1)
    def _epilogue():
        x2 = x1_ref[...] + acc_scr[...].T
        o_ref[...] = _rms(x2, gf_ref[...])


def _peer(h2, sc_t, stats, x1, u, v_t, gf):
    n, d = x1.shape
    nsc = sc_t.shape[0]
    tt, te = TT_PEER, TE_PEER
    return pl.pallas_call(
        _peer_kernel,
        grid=(n // tt, PEER_EXPERTS // te),
        in_specs=[
            pl.BlockSpec((tt, d), lambda t, e: (t, 0)),
            pl.BlockSpec((nsc, tt), lambda t, e: (0, t)),
            pl.BlockSpec((4 * PEER_HEADS, tt), lambda t, e: (0, t)),
            pl.BlockSpec((tt, d), lambda t, e: (t, 0)),
            pl.BlockSpec((te, d), lambda t, e: (e, 0)),
            pl.BlockSpec((d, te), lambda t, e: (0, e)),
            pl.BlockSpec((1, d), lambda t, e: (0, 0)),
        ],
        out_specs=pl.BlockSpec((tt, d), lambda t, e: (t, 0)),
        out_shape=jax.ShapeDtypeStruct((n, d), F32),
        scratch_shapes=[
            pltpu.VMEM((PEER_HEADS, PEER_NKEYS, tt), F32),
            pltpu.VMEM((PEER_HEADS, PEER_NKEYS, tt), F32),
            pltpu.VMEM((PEER_HEADS, PEER_NKEYS, tt), F32),
            pltpu.VMEM((PEER_HEADS, PEER_NKEYS, tt), F32),
            pltpu.VMEM((te, tt), F32),
            pltpu.VMEM((te, tt), BF16),
            pltpu.VMEM((d, tt), F32),
        ],
        compiler_params=pltpu.CompilerParams(
            dimension_semantics=("parallel", "arbitrary"), vmem_limit_bytes=VMEM_LIMIT),
        name="peer",
    )(h2, sc_t, stats, x1, u, v_t, gf)


def _fourier_weights(w_in_f, w_fourier, seq):
    kc = jnp.arange(FOURIER_GROUP, dtype=jnp.int32)
    ang_c = (2.0 * math.pi / FOURIER_GROUP) * ((kc[:, None] * kc[None, :]) % FOURIER_GROUP).astype(F32)
    scale = 1.0 / math.sqrt(seq * FOURIER_GROUP)
    wf = w_fourier.astype(F32)
    wc = jnp.einsum("ck,gkd->gcd", jnp.cos(ang_c), wf, precision=_HI) * scale
    ws = jnp.einsum("ck,gkd->gcd", jnp.sin(ang_c), wf, precision=_HI) * scale
    w_in_g = w_in_f.astype(F32).reshape(D_MODEL, N_FOURIER_GROUPS, FOURIER_GROUP)
    w_p = jnp.einsum("mgc,gcd->mgd", w_in_g, wc, precision=_HI).reshape(D_MODEL, D_FNO)
    w_q = jnp.einsum("mgc,gcd->mgd", w_in_g, ws, precision=_HI).reshape(D_MODEL, D_FNO)
    ks = jnp.arange(seq, dtype=jnp.int32)
    ang_s = (2.0 * math.pi / seq) * ((ks[:, None] * ks[None, :]) % seq).astype(F32)
    return w_p, w_q, jnp.cos(ang_s).astype(BF16), (-jnp.sin(ang_s)).astype(BF16)


def kernel(x, norm1_g, w_in, ssm_a_re, ssm_a_im, ssm_log_step, ssm_b_re, ssm_b_im, ssm_c_re,
           ssm_c_im, ssm_d, w_glu, w_fourier, w_out, norm2_g, w_query, sub_keys, expert_u,
           expert_v, final_g):
    bsz, seq, d = x.shape
    n = bsz * seq
    n_chunks = seq // CHUNK
    assert d == D_MODEL and seq % TM_IN == 0 and seq % TK_DFT == 0 and n % TT_PEER == 0
    assert norm1_g.shape[0] == 1, "single-layer block"
    layer = 0

    w_p, w_q, cos_t, nsin_t = _fourier_weights(w_in[layer][:, D_SSM:], w_fourier[layer], seq)
    w_all = jnp.concatenate([w_in[layer][:, :D_SSM].astype(F32), w_p, w_q], axis=1).astype(BF16)
    wb, t_mat, wca, wcb, lam16 = _s5_weights(
        ssm_a_re[layer], ssm_a_im[layer], ssm_log_step[layer], ssm_b_re[layer], ssm_b_im[layer],
        ssm_c_re[layer], ssm_c_im[layer], ssm_d[layer])
    wq4 = w_query[layer].astype(F32).reshape(D_MODEL, PEER_HEADS, 2, PEER_HALF)
    wsc_t = jnp.einsum("dhpe,hpke->hpkd", wq4, sub_keys[layer].astype(F32), precision=_HI)
    wsc_t = wsc_t.reshape(PEER_HEADS * 2 * PEER_NKEYS, D_MODEL).astype(BF16)
    u_bf = expert_u[layer].astype(BF16)
    v_t = expert_v[layer].astype(BF16).T

    u, p, q = _inproj(x, norm1_g[layer].reshape(1, d).astype(F32), w_all)

    xg = u.reshape(bsz, n_chunks, CHUNK, N_SSM_GROUPS, SSM_GROUP).transpose(3, 1, 0, 2, 4)
    xg = xg.reshape(N_SSM_GROUPS, n_chunks * bsz, CHUNK_W)
    yg = _s5(xg, wb, t_mat, wca, wcb, lam16, n_chunks, bsz)
    y = yg.reshape(N_SSM_GROUPS, n_chunks, bsz, CHUNK, SSM_GROUP).transpose(2, 1, 3, 0, 4)
    y = y.reshape(n, D_SSM)

    f = _seq_dft(cos_t, nsin_t, p, q, bsz).reshape(n, D_FNO)

    x1, h2, sc_t = _merge(x.reshape(n, d), y, f, w_glu[layer].astype(BF16),
                          w_out[layer].astype(BF16), norm2_g[layer].reshape(1, d).astype(F32), wsc_t)

    stats = _topk_stats(sc_t)
    out = _peer(h2, sc_t, stats, x1, u_bf, v_t, final_g.reshape(1, d).astype(F32))
    return out.reshape(bsz, seq, d)
```

```python
import functools
import math

import jax
import jax.numpy as jnp
from jax import lax
from jax.experimental import pallas as pl
from jax.experimental.pallas import tpu as pltpu

F32 = jnp.float32
BF16 = jnp.bfloat16

D_MODEL = 1024
D_SSM = 512
D_FNO = 512
SSM_GROUP = 16
N_SSM_GROUPS = 32
SSM_STATE = 64
N_FOURIER_GROUPS = 4
FOURIER_GROUP = 128
PEER_HEADS = 8
PEER_NKEYS = 128
PEER_EXPERTS = PEER_NKEYS * PEER_NKEYS
PEER_TOPK = 16
PEER_HALF = 128
RMS_EPS = 1e-6

CHUNK = 16
CHUNK_W = CHUNK * SSM_GROUP
LANES = 128
VMEM_LIMIT = 56 * 1024 * 1024

TM_IN = 512
TK_DFT = 512
TT_PEER = 512
SUBLANES = 8
BF16_ROWS = 16
PEER_LANE_CHUNK = 256
TE_PEER = SUBLANES * PEER_NKEYS

_HI = lax.Precision.HIGHEST


def _rms(x, g):
    return x * lax.rsqrt(jnp.mean(x * x, axis=-1, keepdims=True) + RMS_EPS) * g


def _gelu(x):
    return 0.5 * x * (1.0 + lax.erf(x * math.sqrt(0.5)))


def _inproj_kernel(x_ref, g_ref, w_ref, u_ref, p_ref, q_ref):
    h = _rms(x_ref[...], g_ref[...])
    z = jnp.dot(h.astype(BF16), w_ref[...], preferred_element_type=F32)
    u_ref[...] = z[:, :D_SSM].astype(BF16)
    p_ref[...] = z[:, D_SSM:D_SSM + D_FNO].astype(BF16)
    q_ref[...] = z[:, D_SSM + D_FNO:].astype(BF16)


def _inproj(x, g, w_all):
    bsz, seq, d = x.shape
    nw = w_all.shape[1]
    return pl.pallas_call(
        _inproj_kernel,
        grid=(bsz, seq // TM_IN),
        in_specs=[
            pl.BlockSpec((None, TM_IN, d), lambda b, i: (b, i, 0)),
            pl.BlockSpec((1, d), lambda b, i: (0, 0)),
            pl.BlockSpec((d, nw), lambda b, i: (0, 0)),
        ],
        out_specs=[
            pl.BlockSpec((None, TM_IN, D_SSM), lambda b, i: (b, i, 0)),
            pl.BlockSpec((TM_IN, D_FNO), lambda b, i: (i, b)),
            pl.BlockSpec((TM_IN, D_FNO), lambda b, i: (i, b)),
        ],
        out_shape=[
            jax.ShapeDtypeStruct((bsz, seq, D_SSM), BF16),
            jax.ShapeDtypeStruct((seq, bsz * D_FNO), BF16),
            jax.ShapeDtypeStruct((seq, bsz * D_FNO), BF16),
        ],
        compiler_params=pltpu.CompilerParams(
            dimension_semantics=("parallel", "parallel"), vmem_limit_bytes=VMEM_LIMIT),
        name="inproj",
    )(x, g, w_all)


def _s5_kernel(x_ref, wb_ref, t_ref, wca_ref, wcb_ref, lam_ref, y_ref, s_scr, ha_scr, hb_scr,
               *, n_chunks, bsz):
    rows = n_chunks * bsz
    rblk = min(rows, 512)
    def summarize(r, c):
        rs = pl.ds(pl.multiple_of(r * rblk, rblk), rblk)
        s_scr[rs, :] = jnp.dot(x_ref[rs, :], wb_ref[...], preferred_element_type=F32)
        return c
    lax.fori_loop(0, rows // rblk, summarize, 0)

    lr = lam_ref[0:1, :]
    li = lam_ref[1:2, :]
    is_fwd = lax.broadcasted_iota(jnp.int32, (bsz, LANES), 1) < SSM_STATE

    def step(k, carry):
        hre, him = carry
        rf = pl.ds(pl.multiple_of(k * bsz, bsz), bsz)
        rb = pl.ds(pl.multiple_of((n_chunks - 1 - k) * bsz, bsz), bsz)
        ha_scr[rf, 0:LANES] = hre
        ha_scr[rf, LANES:2 * LANES] = him
        hb_scr[rb, 0:LANES] = hre
        hb_scr[rb, LANES:2 * LANES] = him
        sre = jnp.where(is_fwd, s_scr[rf, 0:LANES], s_scr[rb, 0:LANES])
        sim = jnp.where(is_fwd, s_scr[rf, LANES:2 * LANES], s_scr[rb, LANES:2 * LANES])
        nre = lr * hre - li * him + sre
        nim = lr * him + li * hre + sim
        return nre, nim
    zero = jnp.zeros((bsz, LANES), F32)
    lax.fori_loop(0, n_chunks, step, (zero, zero), unroll=4)

    def emit(r, c):
        rs = pl.ds(pl.multiple_of(r * rblk, rblk), rblk)
        y = jnp.dot(x_ref[rs, :], t_ref[...], preferred_element_type=F32)
        y = y + jnp.dot(ha_scr[rs, :].astype(BF16), wca_ref[...], preferred_element_type=F32)
        y = y + jnp.dot(hb_scr[rs, :].astype(BF16), wcb_ref[...], preferred_element_type=F32)
        y_ref[rs, :] = y.astype(BF16)
        return c
    lax.fori_loop(0, rows // rblk, emit, 0)


def _s5(xg, wb, tt, wca, wcb, lam16, n_chunks, bsz):
    ng, rows, w = xg.shape
    wspec = pl.BlockSpec((None, w, w), lambda g: (g, 0, 0))
    return pl.pallas_call(
        functools.partial(_s5_kernel, n_chunks=n_chunks, bsz=bsz),
        grid=(ng,),
        in_specs=[
            pl.BlockSpec((None, rows, w), lambda g: (g, 0, 0)),
            wspec, wspec, wspec, wspec,
            pl.BlockSpec((None, 2, LANES), lambda g: (g, 0, 0)),
        ],
        out_specs=pl.BlockSpec((None, rows, w), lambda g: (g, 0, 0)),
        out_shape=jax.ShapeDtypeStruct((ng, rows, w), BF16),
        scratch_shapes=[
            pltpu.VMEM((rows, w), F32),
            pltpu.VMEM((rows, w), F32),
            pltpu.VMEM((rows, w), F32),
        ],
        compiler_params=pltpu.CompilerParams(
            dimension_semantics=("parallel",), vmem_limit_bytes=VMEM_LIMIT),
        name="s5",
    )(xg, wb, tt, wca, wcb, lam16)


def _s5_weights(a_re, a_im, log_step, b_re, b_im, c_re, c_im, d_skip):
    g_n, p_n, cg = N_SSM_GROUPS, SSM_STATE, SSM_GROUP
    tau = jnp.arange(CHUNK + 1, dtype=F32)

    def direction(d):
        lam = lax.complex(a_re[d].astype(F32), a_im[d].astype(F32))
        step = jnp.exp(log_step[d].astype(F32))[:, None]
        lam_step = lam * step
        lam_bar = jnp.exp(lam_step)
        b_bar = ((lam_bar - 1.0) / lam)[..., None] * lax.complex(b_re[d].astype(F32), b_im[d].astype(F32))
        c = lax.complex(c_re[d].astype(F32), c_im[d].astype(F32))
        lam_pow = jnp.exp(lam_step[None] * tau[:, None, None])
        kern = jnp.einsum("gop,tgp,gpi->tgoi", c, lam_pow[:CHUNK], b_bar, precision=_HI).real
        return lam_pow, b_bar, c, kern

    lp_f, bb_f, c_f, k_f = direction(0)
    lp_b, bb_b, c_b, k_b = direction(1)

    t_idx = jnp.arange(CHUNK)
    lag = t_idx[None, :] - t_idx[:, None]
    kf = jnp.where((lag >= 0)[:, :, None, None, None], k_f[jnp.clip(lag, 0, CHUNK - 1)], 0.0)
    kb = jnp.where((lag <= 0)[:, :, None, None, None], k_b[jnp.clip(-lag, 0, CHUNK - 1)], 0.0)
    eye_t = jnp.eye(CHUNK, dtype=F32)[:, :, None, None, None]
    eye_c = jnp.eye(cg, dtype=F32)[None, None, None]
    dmat = eye_t * eye_c * d_skip.astype(F32)[None, None, :, :, None]
    tot = kf + kb + dmat
    t_mat = tot.transpose(2, 0, 4, 1, 3).reshape(g_n, CHUNK_W, CHUNK_W)

    wb_f = jnp.einsum("tgp,gpi->gtip", lp_f[:CHUNK][::-1], bb_f).reshape(g_n, CHUNK_W, p_n)
    wb_b = jnp.einsum("tgp,gpi->gtip", lp_b[:CHUNK], bb_b).reshape(g_n, CHUNK_W, p_n)
    wb = jnp.concatenate([wb_f.real, wb_b.real, wb_f.imag, wb_b.imag], axis=-1)

    wc_f = jnp.einsum("gop,tgp->gpto", c_f, lp_f[1:CHUNK + 1]).reshape(g_n, p_n, CHUNK_W)
    wc_b = jnp.einsum("gop,tgp->gpto", c_b, lp_b[1:CHUNK + 1][::-1]).reshape(g_n, p_n, CHUNK_W)
    zeros = jnp.zeros_like(wc_f.real)
    wca = jnp.concatenate([wc_f.real, zeros, -wc_f.imag, zeros], axis=1)
    wcb = jnp.concatenate([zeros, wc_b.real, zeros, -wc_b.imag], axis=1)

    lam16 = jnp.stack([
        jnp.concatenate([lp_f[CHUNK].real, lp_b[CHUNK].real], axis=-1),
        jnp.concatenate([lp_f[CHUNK].imag, lp_b[CHUNK].imag], axis=-1)], axis=1)
    return wb.astype(BF16), t_mat.astype(BF16), wca.astype(BF16), wcb.astype(BF16), lam16


def _dft_kernel(c_ref, s_ref, p_ref, q_ref, o_ref):
    acc = jnp.dot(c_ref[...], p_ref[...], preferred_element_type=F32)
    acc = acc + jnp.dot(s_ref[...], q_ref[...], preferred_element_type=F32)
    o_ref[...] = acc.astype(BF16)


def _seq_dft(cos_t, nsin_t, p, q, bsz):
    seq = cos_t.shape[0]
    return pl.pallas_call(
        _dft_kernel,
        grid=(seq // TK_DFT, bsz),
        in_specs=[
            pl.BlockSpec((TK_DFT, seq), lambda i, b: (i, 0)),
            pl.BlockSpec((TK_DFT, seq), lambda i, b: (i, 0)),
            pl.BlockSpec((seq, D_FNO), lambda i, b: (0, b)),
            pl.BlockSpec((seq, D_FNO), lambda i, b: (0, b)),
        ],
        out_specs=pl.BlockSpec((None, TK_DFT, D_FNO), lambda i, b: (b, i, 0)),
        out_shape=jax.ShapeDtypeStruct((bsz, seq, D_FNO), BF16),
        compiler_params=pltpu.CompilerParams(
            dimension_semantics=("parallel", "arbitrary"), vmem_limit_bytes=VMEM_LIMIT),
        name="seq_dft",
    )(cos_t, nsin_t, p, q)


def _merge_kernel(x_ref, y_ref, f_ref, wglu_ref, wout_ref, g2_ref, wsc_ref,
                  x1_ref, h2_ref, sc_ref):
    yg = _gelu(y_ref[...].astype(F32))
    gate = jax.nn.sigmoid(jnp.dot(yg.astype(BF16), wglu_ref[...], preferred_element_type=F32))
    ys = (yg * gate).astype(BF16)
    mix = jnp.dot(ys, wout_ref[0:D_SSM, :], preferred_element_type=F32)
    mix = mix + jnp.dot(f_ref[...], wout_ref[D_SSM:, :], preferred_element_type=F32)
    x1 = x_ref[...] + mix
    x1_ref[...] = x1
    h2 = _rms(x1, g2_ref[...]).astype(BF16)
    h2_ref[...] = h2
    sc_ref[...] = lax.dot_general(wsc_ref[...], h2, (((1,), (1,)), ((), ())),
                                  preferred_element_type=F32)


def _merge(x2, y, f, wglu, wout, g2, wsc_t):
    n, d = x2.shape
    nsc = wsc_t.shape[0]
    tm = TM_IN
    return pl.pallas_call(
        _merge_kernel,
        grid=(n // tm,),
        in_specs=[
            pl.BlockSpec((tm, d), lambda i: (i, 0)),
            pl.BlockSpec((tm, D_SSM), lambda i: (i, 0)),
            pl.BlockSpec((tm, D_FNO), lambda i: (i, 0)),
            pl.BlockSpec((D_SSM, D_SSM), lambda i: (0, 0)),
            pl.BlockSpec((d, d), lambda i: (0, 0)),
            pl.BlockSpec((1, d), lambda i: (0, 0)),
            pl.BlockSpec((nsc, d), lambda i: (0, 0)),
        ],
        out_specs=[
            pl.BlockSpec((tm, d), lambda i: (i, 0)),
            pl.BlockSpec((tm, d), lambda i: (i, 0)),
            pl.BlockSpec((nsc, tm), lambda i: (0, i)),
        ],
        out_shape=[
            jax.ShapeDtypeStruct((n, d), F32),
            jax.ShapeDtypeStruct((n, d), BF16),
            jax.ShapeDtypeStruct((nsc, n), F32),
        ],
        compiler_params=pltpu.CompilerParams(
            dimension_semantics=("parallel",), vmem_limit_bytes=VMEM_LIMIT),
        name="merge",
    )(x2, y, f, wglu, wout, g2, wsc_t)


_PARTIAL_ROWS = ((2, 5), (3, 4), (4, 3), (5, 2), (6, 2), (7, 2))


def _dup_bf16(v):
    hi = pltpu.bitcast(v.astype(BF16).astype(F32), jnp.uint32)
    return hi | (hi >> 16)


def _gate_kernel(sc_ref, al_ref, ll_ref, be_ref, r2_ref, v_scr):
    neg = -jnp.inf
    riota = lax.broadcasted_iota(jnp.int32, (8, LANES), 0)

    def per_head(h, c):
        s1 = sc_ref[pl.ds(pl.multiple_of(2 * h * PEER_NKEYS, PEER_NKEYS), PEER_NKEYS), :]
        s2 = sc_ref[pl.ds(pl.multiple_of((2 * h + 1) * PEER_NKEYS, PEER_NKEYS), PEER_NKEYS), :]
        out_rows = pl.ds(pl.multiple_of(h * PEER_NKEYS, PEER_NKEYS), PEER_NKEYS)
        s = s1
        for k in range(PEER_TOPK):
            m = jnp.max(s, axis=0, keepdims=True)
            v_scr[0, k:k + 1, :] = m
            s = jnp.where(s == m, neg, s)
        s = s2
        rank2 = jnp.full((PEER_NKEYS, LANES), float(PEER_TOPK), F32)
        for k in range(PEER_TOPK):
            m = jnp.max(s, axis=0, keepdims=True)
            v_scr[1, k:k + 1, :] = m
            hit = s == m
            rank2 = jnp.where(hit, float(k), rank2)
            s = jnp.where(hit, neg, s)
        v1a = v_scr[0, 0:8, :]
        v1b = v_scr[0, 8:16, :]
        v2a = v_scr[1, 0:8, :]
        v2b = v_scr[1, 8:16, :]
        top1 = v_scr[0, 0:1, :]
        top2 = v_scr[1, 0:1, :]
        blocks = [top1 + v2a, top1 + v2b, v_scr[0, 1:2, :] + v2a]
        for a, nb in _PARTIAL_ROWS:
            blocks.append(jnp.where(riota < nb, v_scr[0, a:a + 1, :] + v2a, neg))
        blocks.append(v1b + top2)
        cand = jnp.concatenate(blocks, axis=0)
        mtop = top1 + top2
        z = jnp.zeros((1, LANES), F32)
        tau = mtop
        for k in range(PEER_TOPK):
            m = jnp.max(cand, axis=0, keepdims=True)
            live = m > neg
            z = z + jnp.where(live, jnp.exp(m - mtop), 0.0)
            tau = jnp.where(live, m, tau)
            cand = jnp.where(cand == m, neg, cand)
        len1 = jnp.zeros((PEER_NKEYS, LANES), F32)
        for b in range(PEER_TOPK):
            len1 = jnp.where((s1 + v_scr[1, b:b + 1, :]) >= tau, float(b + 1), len1)
        al_ref[out_rows, :] = _dup_bf16(jnp.exp(s1 - top1) * (1.0 / z))
        ll_ref[out_rows, :] = _dup_bf16(len1)
        be_ref[out_rows, :] = jnp.exp(s2 - top2).astype(BF16)
        r2_ref[out_rows, :] = rank2.astype(BF16)
        return c
    lax.fori_loop(0, PEER_HEADS, per_head, 0)


def _gate_prep(sc_t):
    nsc, n = sc_t.shape
    rows = PEER_HEADS * PEER_NKEYS
    spec = pl.BlockSpec((rows, LANES), lambda i: (0, i))
    return pl.pallas_call(
        _gate_kernel,
        grid=(n // LANES,),
        in_specs=[pl.BlockSpec((nsc, LANES), lambda i: (0, i))],
        out_specs=[spec, spec, spec, spec],
        out_shape=[
            jax.ShapeDtypeStruct((rows, n), jnp.uint32),
            jax.ShapeDtypeStruct((rows, n), jnp.uint32),
            jax.ShapeDtypeStruct((rows, n), BF16),
            jax.ShapeDtypeStruct((rows, n), BF16),
        ],
        scratch_shapes=[pltpu.VMEM((2, PEER_TOPK, LANES), F32)],
        compiler_params=pltpu.CompilerParams(
            dimension_semantics=("parallel",), vmem_limit_bytes=VMEM_LIMIT),
        name="gate_prep",
    )(sc_t)


def _peer_kernel(h2_ref, x1_ref, al_ref, ll_ref, be_ref, r2_ref, u_ref, vt_ref, gf_ref, o_ref,
                 pre_scr, a_scr, acc_scr):
    e = pl.program_id(1)
    n_e = pl.num_programs(1)
    rows_per_tile = TE_PEER // PEER_NKEYS
    n_lane_chunks = TT_PEER // PEER_LANE_CHUNK
    packed = (PEER_NKEYS // BF16_ROWS, BF16_ROWS, PEER_LANE_CHUNK)

    @pl.when(e == 0)
    def _init():
        acc_scr[...] = jnp.zeros_like(acc_scr)

    pre_scr[...] = lax.dot_general(u_ref[...], h2_ref[...], (((1,), (1,)), ((), ())),
                                   preferred_element_type=F32)

    for lc in range(n_lane_chunks):
        ls = slice(lc * PEER_LANE_CHUNK, (lc + 1) * PEER_LANE_CHUNK)
        al_blk, ll_blk = [], []
        for h in range(PEER_HEADS):
            i_rows = pl.ds(pl.multiple_of(h * PEER_NKEYS + e * rows_per_tile, rows_per_tile),
                           rows_per_tile)
            al_blk.append(al_ref[i_rows, ls])
            ll_blk.append(ll_ref[i_rows, ls])
        for il in range(rows_per_tile):
            rs = slice(il * PEER_NKEYS, (il + 1) * PEER_NKEYS)
            g = jnp.zeros(packed, BF16)
            for h in range(PEER_HEADS):
                hs = slice(h * PEER_NKEYS, (h + 1) * PEER_NKEYS)
                len1 = pltpu.bitcast(jnp.broadcast_to(ll_blk[h][il:il + 1, :], (SUBLANES, PEER_LANE_CHUNK)), BF16)
                alpha = pltpu.bitcast(jnp.broadcast_to(al_blk[h][il:il + 1, :], (SUBLANES, PEER_LANE_CHUNK)), BF16)
                keep = r2_ref[hs, ls].reshape(packed) < len1[None]
                g = g + jnp.where(keep, be_ref[hs, ls].reshape(packed) * alpha[None], 0.0)
            act = _gelu(pre_scr[rs, ls]).astype(BF16).reshape(packed) * g
            a_scr[rs, ls] = act.reshape(PEER_NKEYS, PEER_LANE_CHUNK)

    acc_scr[...] += jnp.dot(vt_ref[...], a_scr[...], preferred_element_type=F32)

    @pl.when(e == n_e - 1)
    def _epilogue():
        x2 = x1_ref[...] + acc_scr[...].T
        o_ref[...] = _rms(x2, gf_ref[...])


def _peer(h2, x1, al, ll, be, r2, u, v_t, gf):
    n, d = x1.shape
    rows = PEER_HEADS * PEER_NKEYS
    tt, te = TT_PEER, TE_PEER
    gate_spec = pl.BlockSpec((rows, tt), lambda t, e: (0, t))
    return pl.pallas_call(
        _peer_kernel,
        grid=(n // tt, PEER_EXPERTS // te),
        in_specs=[
            pl.BlockSpec((tt, d), lambda t, e: (t, 0)),
            pl.BlockSpec((tt, d), lambda t, e: (t, 0)),
            gate_spec, gate_spec, gate_spec, gate_spec,
            pl.BlockSpec((te, d), lambda t, e: (e, 0)),
            pl.BlockSpec((d, te), lambda t, e: (0, e)),
            pl.BlockSpec((1, d), lambda t, e: (0, 0)),
        ],
        out_specs=pl.BlockSpec((tt, d), lambda t, e: (t, 0)),
        out_shape=jax.ShapeDtypeStruct((n, d), F32),
        scratch_shapes=[
            pltpu.VMEM((te, tt), F32),
            pltpu.VMEM((te, tt), BF16),
            pltpu.VMEM((d, tt), F32),
        ],
        compiler_params=pltpu.CompilerParams(
            dimension_semantics=("parallel", "arbitrary"), vmem_limit_bytes=VMEM_LIMIT),
        name="peer",
    )(h2, x1, al, ll, be, r2, u, v_t, gf)


def _fourier_weights(w_in_f, w_fourier, seq):
    kc = jnp.arange(FOURIER_GROUP, dtype=jnp.int32)
    ang_c = (2.0 * math.pi / FOURIER_GROUP) * ((kc[:, None] * kc[None, :]) % FOURIER_GROUP).astype(F32)
    scale = 1.0 / math.sqrt(seq * FOURIER_GROUP)
    wf = w_fourier.astype(F32)
    wc = jnp.einsum("ck,gkd->gcd", jnp.cos(ang_c), wf, precision=_HI) * scale
    ws = jnp.einsum("ck,gkd->gcd", jnp.sin(ang_c), wf, precision=_HI) * scale
    w_in_g = w_in_f.astype(F32).reshape(D_MODEL, N_FOURIER_GROUPS, FOURIER_GROUP)
    w_p = jnp.einsum("mgc,gcd->mgd", w_in_g, wc, precision=_HI).reshape(D_MODEL, D_FNO)
    w_q = jnp.einsum("mgc,gcd->mgd", w_in_g, ws, precision=_HI).reshape(D_MODEL, D_FNO)
    ks = jnp.arange(seq, dtype=jnp.int32)
    ang_s = (2.0 * math.pi / seq) * ((ks[:, None] * ks[None, :]) % seq).astype(F32)
    return w_p, w_q, jnp.cos(ang_s).astype(BF16), (-jnp.sin(ang_s)).astype(BF16)


def kernel(x, norm1_g, w_in, ssm_a_re, ssm_a_im, ssm_log_step, ssm_b_re, ssm_b_im, ssm_c_re,
           ssm_c_im, ssm_d, w_glu, w_fourier, w_out, norm2_g, w_query, sub_keys, expert_u,
           expert_v, final_g):
    bsz, seq, d = x.shape
    n = bsz * seq
    n_chunks = seq // CHUNK
    assert d == D_MODEL and seq % TM_IN == 0 and seq % TK_DFT == 0 and n % TT_PEER == 0
    assert norm1_g.shape[0] == 1, "single-layer block"
    layer = 0

    w_p, w_q, cos_t, nsin_t = _fourier_weights(w_in[layer][:, D_SSM:], w_fourier[layer], seq)
    w_all = jnp.concatenate([w_in[layer][:, :D_SSM].astype(F32), w_p, w_q], axis=1).astype(BF16)
    wb, t_mat, wca, wcb, lam16 = _s5_weights(
        ssm_a_re[layer], ssm_a_im[layer], ssm_log_step[layer], ssm_b_re[layer], ssm_b_im[layer],
        ssm_c_re[layer], ssm_c_im[layer], ssm_d[layer])
    wq4 = w_query[layer].astype(F32).reshape(D_MODEL, PEER_HEADS, 2, PEER_HALF)
    wsc_t = jnp.einsum("dhpe,hpke->hpkd", wq4, sub_keys[layer].astype(F32), precision=_HI)
    wsc_t = wsc_t.reshape(PEER_HEADS * 2 * PEER_NKEYS, D_MODEL).astype(BF16)
    u_bf = expert_u[layer].astype(BF16)
    v_t = expert_v[layer].astype(BF16).T

    u, p, q = _inproj(x, norm1_g[layer].reshape(1, d).astype(F32), w_all)

    xg = u.reshape(bsz, n_chunks, CHUNK, N_SSM_GROUPS, SSM_GROUP).transpose(3, 1, 0, 2, 4)
    xg = xg.reshape(N_SSM_GROUPS, n_chunks * bsz, CHUNK_W)
    yg = _s5(xg, wb, t_mat, wca, wcb, lam16, n_chunks, bsz)
    y = yg.reshape(N_SSM_GROUPS, n_chunks, bsz, CHUNK, SSM_GROUP).transpose(2, 1, 3, 0, 4)
    y = y.reshape(n, D_SSM)

    f = _seq_dft(cos_t, nsin_t, p, q, bsz).reshape(n, D_FNO)

    x1, h2, sc_t = _merge(x.reshape(n, d), y, f, w_glu[layer].astype(BF16),
                          w_out[layer].astype(BF16), norm2_g[layer].reshape(1, d).astype(F32), wsc_t)

    al, ll, be, r2 = _gate_prep(sc_t)
    out = _peer(h2, x1, al, ll, be, r2, u_bf, v_t, final_g.reshape(1, d).astype(F32))
    return out.reshape(bsz, seq, d)
```

```python
import functools
import math

import jax
import jax.numpy as jnp
from jax import lax
from jax.experimental import pallas as pl
from jax.experimental.pallas import tpu as pltpu

F32 = jnp.float32
BF16 = jnp.bfloat16

D_MODEL = 1024
D_SSM = 512
D_FNO = 512
SSM_GROUP = 16
N_SSM_GROUPS = 32
SSM_STATE = 64
N_FOURIER_GROUPS = 4
FOURIER_GROUP = 128
PEER_HEADS = 8
PEER_NKEYS = 128
PEER_EXPERTS = PEER_NKEYS * PEER_NKEYS
PEER_TOPK = 16
PEER_HALF = 128
RMS_EPS = 1e-6

CHUNK = 16
CHUNK_W = CHUNK * SSM_GROUP
LANES = 128
VMEM_LIMIT = 56 * 1024 * 1024

TM_IN = 512
TK_DFT = 512
TT_PEER = 512
SUBLANES = 8
BF16_ROWS = 16
PEER_LANE_CHUNK = 256
TE_PEER = 2 * SUBLANES * PEER_NKEYS

_HI = lax.Precision.HIGHEST


def _rms(x, g):
    return x * lax.rsqrt(jnp.mean(x * x, axis=-1, keepdims=True) + RMS_EPS) * g


def _gelu(x):
    return 0.5 * x * (1.0 + lax.erf(x * math.sqrt(0.5)))


def _inproj_kernel(x_ref, g_ref, w_ref, u_ref, p_ref, q_ref):
    h = _rms(x_ref[...], g_ref[...])
    z = jnp.dot(h.astype(BF16), w_ref[...], preferred_element_type=F32)
    u_ref[...] = z[:, :D_SSM].astype(BF16)
    p_ref[...] = z[:, D_SSM:D_SSM + D_FNO].astype(BF16)
    q_ref[...] = z[:, D_SSM + D_FNO:].astype(BF16)


def _inproj(x, g, w_all):
    bsz, seq, d = x.shape
    nw = w_all.shape[1]
    return pl.pallas_call(
        _inproj_kernel,
        grid=(bsz, seq // TM_IN),
        in_specs=[
            pl.BlockSpec((None, TM_IN, d), lambda b, i: (b, i, 0)),
            pl.BlockSpec((1, d), lambda b, i: (0, 0)),
            pl.BlockSpec((d, nw), lambda b, i: (0, 0)),
        ],
        out_specs=[
            pl.BlockSpec((None, TM_IN, D_SSM), lambda b, i: (b, i, 0)),
            pl.BlockSpec((TM_IN, D_FNO), lambda b, i: (i, b)),
            pl.BlockSpec((TM_IN, D_FNO), lambda b, i: (i, b)),
        ],
        out_shape=[
            jax.ShapeDtypeStruct((bsz, seq, D_SSM), BF16),
            jax.ShapeDtypeStruct((seq, bsz * D_FNO), BF16),
            jax.ShapeDtypeStruct((seq, bsz * D_FNO), BF16),
        ],
        compiler_params=pltpu.CompilerParams(
            dimension_semantics=("parallel", "parallel"), vmem_limit_bytes=VMEM_LIMIT),
        name="inproj",
    )(x, g, w_all)


def _s5_kernel(x_ref, wb_ref, t_ref, wca_ref, wcb_ref, lam_ref, y_ref, s_scr, ha_scr, hb_scr,
               *, n_chunks, bsz):
    rows = n_chunks * bsz
    rblk = min(rows, 512)
    def summarize(r, c):
        rs = pl.ds(pl.multiple_of(r * rblk, rblk), rblk)
        s_scr[rs, :] = jnp.dot(x_ref[rs, :], wb_ref[...], preferred_element_type=F32)
        return c
    lax.fori_loop(0, rows // rblk, summarize, 0)

    lr = lam_ref[0:1, :]
    li = lam_ref[1:2, :]
    is_fwd = lax.broadcasted_iota(jnp.int32, (bsz, LANES), 1) < SSM_STATE

    def step(k, carry):
        hre, him = carry
        rf = pl.ds(pl.multiple_of(k * bsz, bsz), bsz)
        rb = pl.ds(pl.multiple_of((n_chunks - 1 - k) * bsz, bsz), bsz)
        ha_scr[rf, 0:LANES] = hre
        ha_scr[rf, LANES:2 * LANES] = him
        hb_scr[rb, 0:LANES] = hre
        hb_scr[rb, LANES:2 * LANES] = him
        sre = jnp.where(is_fwd, s_scr[rf, 0:LANES], s_scr[rb, 0:LANES])
        sim = jnp.where(is_fwd, s_scr[rf, LANES:2 * LANES], s_scr[rb, LANES:2 * LANES])
        nre = lr * hre - li * him + sre
        nim = lr * him + li * hre + sim
        return nre, nim
    zero = jnp.zeros((bsz, LANES), F32)
    lax.fori_loop(0, n_chunks, step, (zero, zero), unroll=4)

    def emit(r, c):
        rs = pl.ds(pl.multiple_of(r * rblk, rblk), rblk)
        y = jnp.dot(x_ref[rs, :], t_ref[...], preferred_element_type=F32)
        y = y + jnp.dot(ha_scr[rs, :].astype(BF16), wca_ref[...], preferred_element_type=F32)
        y = y + jnp.dot(hb_scr[rs, :].astype(BF16), wcb_ref[...], preferred_element_type=F32)
        y_ref[rs, :] = y.astype(BF16)
        return c
    lax.fori_loop(0, rows // rblk, emit, 0)


def _s5(xg, wb, tt, wca, wcb, lam16, n_chunks, bsz):
    ng, rows, w = xg.shape
    wspec = pl.BlockSpec((None, w, w), lambda g: (g, 0, 0))
    return pl.pallas_call(
        functools.partial(_s5_kernel, n_chunks=n_chunks, bsz=bsz),
        grid=(ng,),
        in_specs=[
            pl.BlockSpec((None, rows, w), lambda g: (g, 0, 0)),
            wspec, wspec, wspec, wspec,
            pl.BlockSpec((None, 2, LANES), lambda g: (g, 0, 0)),
        ],
        out_specs=pl.BlockSpec((None, rows, w), lambda g: (g, 0, 0)),
        out_shape=jax.ShapeDtypeStruct((ng, rows, w), BF16),
        scratch_shapes=[
            pltpu.VMEM((rows, w), F32),
            pltpu.VMEM((rows, w), F32),
            pltpu.VMEM((rows, w), F32),
        ],
        compiler_params=pltpu.CompilerParams(
            dimension_semantics=("parallel",), vmem_limit_bytes=VMEM_LIMIT),
        name="s5",
    )(xg, wb, tt, wca, wcb, lam16)


def _s5_weights(a_re, a_im, log_step, b_re, b_im, c_re, c_im, d_skip):
    g_n, p_n, cg = N_SSM_GROUPS, SSM_STATE, SSM_GROUP
    tau = jnp.arange(CHUNK + 1, dtype=F32)

    def direction(d):
        lam = lax.complex(a_re[d].astype(F32), a_im[d].astype(F32))
        step = jnp.exp(log_step[d].astype(F32))[:, None]
        lam_step = lam * step
        lam_bar = jnp.exp(lam_step)
        b_bar = ((lam_bar - 1.0) / lam)[..., None] * lax.complex(b_re[d].astype(F32), b_im[d].astype(F32))
        c = lax.complex(c_re[d].astype(F32), c_im[d].astype(F32))
        lam_pow = jnp.exp(lam_step[None] * tau[:, None, None])
        kern = jnp.einsum("gop,tgp,gpi->tgoi", c, lam_pow[:CHUNK], b_bar, precision=_HI).real
        return lam_pow, b_bar, c, kern

    lp_f, bb_f, c_f, k_f = direction(0)
    lp_b, bb_b, c_b, k_b = direction(1)

    t_idx = jnp.arange(CHUNK)
    lag = t_idx[None, :] - t_idx[:, None]
    kf = jnp.where((lag >= 0)[:, :, None, None, None], k_f[jnp.clip(lag, 0, CHUNK - 1)], 0.0)
    kb = jnp.where((lag <= 0)[:, :, None, None, None], k_b[jnp.clip(-lag, 0, CHUNK - 1)], 0.0)
    eye_t = jnp.eye(CHUNK, dtype=F32)[:, :, None, None, None]
    eye_c = jnp.eye(cg, dtype=F32)[None, None, None]
    dmat = eye_t * eye_c * d_skip.astype(F32)[None, None, :, :, None]
    tot = kf + kb + dmat
    t_mat = tot.transpose(2, 0, 4, 1, 3).reshape(g_n, CHUNK_W, CHUNK_W)

    wb_f = jnp.einsum("tgp,gpi->gtip", lp_f[:CHUNK][::-1], bb_f).reshape(g_n, CHUNK_W, p_n)
    wb_b = jnp.einsum("tgp,gpi->gtip", lp_b[:CHUNK], bb_b).reshape(g_n, CHUNK_W, p_n)
    wb = jnp.concatenate([wb_f.real, wb_b.real, wb_f.imag, wb_b.imag], axis=-1)

    wc_f = jnp.einsum("gop,tgp->gpto", c_f, lp_f[1:CHUNK + 1]).reshape(g_n, p_n, CHUNK_W)
    wc_b = jnp.einsum("gop,tgp->gpto", c_b, lp_b[1:CHUNK + 1][::-1]).reshape(g_n, p_n, CHUNK_W)
    zeros = jnp.zeros_like(wc_f.real)
    wca = jnp.concatenate([wc_f.real, zeros, -wc_f.imag, zeros], axis=1)
    wcb = jnp.concatenate([zeros, wc_b.real, zeros, -wc_b.imag], axis=1)

    lam16 = jnp.stack([
        jnp.concatenate([lp_f[CHUNK].real, lp_b[CHUNK].real], axis=-1),
        jnp.concatenate([lp_f[CHUNK].imag, lp_b[CHUNK].imag], axis=-1)], axis=1)
    return wb.astype(BF16), t_mat.astype(BF16), wca.astype(BF16), wcb.astype(BF16), lam16


def _dft_kernel(p_ref, q_ref, o_ref, cd_scr, sd_scr, ct_scr, st_scr, *, seq):
    i = pl.program_id(0)
    b = pl.program_id(1)
    rblk = BF16_ROWS
    n_idx = lax.broadcasted_iota(jnp.int32, (rblk, seq), 1)
    w0 = 2.0 * math.pi / seq

    @pl.when((i == 0) & (b == 0))
    def _base_tables():
        def rows(r, c):
            rs = pl.ds(pl.multiple_of(r * rblk, rblk), rblk)
            dk = lax.broadcasted_iota(jnp.int32, (rblk, seq), 0) + r * rblk
            ang = w0 * ((n_idx * dk) & (seq - 1)).astype(F32)
            cd_scr[rs, :] = jnp.cos(ang)
            sd_scr[rs, :] = jnp.sin(ang)
            return c
        lax.fori_loop(0, TK_DFT // rblk, rows, 0)

    @pl.when(b == 0)
    def _tile_tables():
        ang0 = w0 * ((n_idx[0:1, :] * (i * TK_DFT)) & (seq - 1)).astype(F32)
        cb = jnp.cos(ang0)
        sb = jnp.sin(ang0)

        def rows(r, c):
            rs = pl.ds(pl.multiple_of(r * rblk, rblk), rblk)
            cd = cd_scr[rs, :]
            sd = sd_scr[rs, :]
            ct_scr[rs, :] = (cb * cd - sb * sd).astype(BF16)
            st_scr[rs, :] = (-(sb * cd + cb * sd)).astype(BF16)
            return c
        lax.fori_loop(0, TK_DFT // rblk, rows, 0)

    acc = jnp.dot(ct_scr[...], p_ref[...], preferred_element_type=F32)
    acc = acc + jnp.dot(st_scr[...], q_ref[...], preferred_element_type=F32)
    o_ref[...] = acc.astype(BF16)


def _seq_dft(p, q, bsz):
    seq = p.shape[0]
    assert seq & (seq - 1) == 0, "angle reduction uses a power-of-two sequence length"
    return pl.pallas_call(
        functools.partial(_dft_kernel, seq=seq),
        grid=(seq // TK_DFT, bsz),
        in_specs=[
            pl.BlockSpec((seq, D_FNO), lambda i, b: (0, b)),
            pl.BlockSpec((seq, D_FNO), lambda i, b: (0, b)),
        ],
        out_specs=pl.BlockSpec((None, TK_DFT, D_FNO), lambda i, b: (b, i, 0)),
        out_shape=jax.ShapeDtypeStruct((bsz, seq, D_FNO), BF16),
        scratch_shapes=[
            pltpu.VMEM((TK_DFT, seq), F32),
            pltpu.VMEM((TK_DFT, seq), F32),
            pltpu.VMEM((TK_DFT, seq), BF16),
            pltpu.VMEM((TK_DFT, seq), BF16),
        ],
        compiler_params=pltpu.CompilerParams(
            dimension_semantics=("arbitrary", "arbitrary"), vmem_limit_bytes=VMEM_LIMIT),
        name="seq_dft",
    )(p, q)


def _merge_kernel(x_ref, y_ref, f_ref, wglu_ref, wout_ref, g2_ref, wsc_ref,
                  x1_ref, h2_ref, sc_ref):
    yg = _gelu(y_ref[...].astype(F32))
    gate = jax.nn.sigmoid(jnp.dot(yg.astype(BF16), wglu_ref[...], preferred_element_type=F32))
    ys = (yg * gate).astype(BF16)
    mix = jnp.dot(ys, wout_ref[0:D_SSM, :], preferred_element_type=F32)
    mix = mix + jnp.dot(f_ref[...], wout_ref[D_SSM:, :], preferred_element_type=F32)
    x1 = x_ref[...] + mix
    x1_ref[...] = x1
    h2t = _rms(x1, g2_ref[...]).T.astype(BF16)
    h2_ref[...] = h2t
    sc_ref[...] = jnp.dot(wsc_ref[...], h2t, preferred_element_type=F32)


def _merge(x2, y, f, wglu, wout, g2, wsc_t):
    n, d = x2.shape
    nsc = wsc_t.shape[0]
    tm = TM_IN
    return pl.pallas_call(
        _merge_kernel,
        grid=(n // tm,),
        in_specs=[
            pl.BlockSpec((tm, d), lambda i: (i, 0)),
            pl.BlockSpec((tm, D_SSM), lambda i: (i, 0)),
            pl.BlockSpec((tm, D_FNO), lambda i: (i, 0)),
            pl.BlockSpec((D_SSM, D_SSM), lambda i: (0, 0)),
            pl.BlockSpec((d, d), lambda i: (0, 0)),
            pl.BlockSpec((1, d), lambda i: (0, 0)),
            pl.BlockSpec((nsc, d), lambda i: (0, 0)),
        ],
        out_specs=[
            pl.BlockSpec((tm, d), lambda i: (i, 0)),
            pl.BlockSpec((d, tm), lambda i: (0, i)),
            pl.BlockSpec((nsc, tm), lambda i: (0, i)),
        ],
        out_shape=[
            jax.ShapeDtypeStruct((n, d), F32),
            jax.ShapeDtypeStruct((d, n), BF16),
            jax.ShapeDtypeStruct((nsc, n), F32),
        ],
        compiler_params=pltpu.CompilerParams(
            dimension_semantics=("parallel",), vmem_limit_bytes=VMEM_LIMIT),
        name="merge",
    )(x2, y, f, wglu, wout, g2, wsc_t)


_PARTIAL_ROWS = ((2, 5), (3, 4), (4, 3), (5, 2), (6, 2), (7, 2))


def _dup_bf16(v):
    hi = pltpu.bitcast(v.astype(BF16).astype(F32), jnp.uint32)
    return hi | (hi >> 16)


def _gate_kernel(sc_ref, al_ref, ll_ref, be_ref, r2_ref, v_scr):
    neg = -jnp.inf
    riota = lax.broadcasted_iota(jnp.int32, (8, LANES), 0)

    def per_head(h, c):
        s1 = sc_ref[pl.ds(pl.multiple_of(2 * h * PEER_NKEYS, PEER_NKEYS), PEER_NKEYS), :]
        s2 = sc_ref[pl.ds(pl.multiple_of((2 * h + 1) * PEER_NKEYS, PEER_NKEYS), PEER_NKEYS), :]
        out_rows = pl.ds(pl.multiple_of(h * PEER_NKEYS, PEER_NKEYS), PEER_NKEYS)
        s = s1
        for k in range(PEER_TOPK):
            m = jnp.max(s, axis=0, keepdims=True)
            v_scr[0, k:k + 1, :] = m
            s = jnp.where(s == m, neg, s)
        s = s2
        rank2 = jnp.full((PEER_NKEYS, LANES), float(PEER_TOPK), F32)
        for k in range(PEER_TOPK):
            m = jnp.max(s, axis=0, keepdims=True)
            v_scr[1, k:k + 1, :] = m
            hit = s == m
            rank2 = jnp.where(hit, float(k), rank2)
            s = jnp.where(hit, neg, s)
        v1a = v_scr[0, 0:8, :]
        v1b = v_scr[0, 8:16, :]
        v2a = v_scr[1, 0:8, :]
        v2b = v_scr[1, 8:16, :]
        top1 = v_scr[0, 0:1, :]
        top2 = v_scr[1, 0:1, :]
        blocks = [top1 + v2a, top1 + v2b, v_scr[0, 1:2, :] + v2a]
        for a, nb in _PARTIAL_ROWS:
            blocks.append(jnp.where(riota < nb, v_scr[0, a:a + 1, :] + v2a, neg))
        blocks.append(v1b + top2)
        cand = jnp.concatenate(blocks, axis=0)
        mtop = top1 + top2
        z = jnp.zeros((1, LANES), F32)
        tau = mtop
        for k in range(PEER_TOPK):
            m = jnp.max(cand, axis=0, keepdims=True)
            live = m > neg
            z = z + jnp.where(live, jnp.exp(m - mtop), 0.0)
            tau = jnp.where(live, m, tau)
            cand = jnp.where(cand == m, neg, cand)
        len1 = jnp.zeros((PEER_NKEYS, LANES), F32)
        for b in range(PEER_TOPK):
            len1 = jnp.where((s1 + v_scr[1, b:b + 1, :]) >= tau, float(b + 1), len1)
        al_ref[out_rows, :] = _dup_bf16(jnp.exp(s1 - top1) * (1.0 / z))
        ll_ref[out_rows, :] = _dup_bf16(len1)
        be_ref[out_rows, :] = jnp.exp(s2 - top2).astype(BF16)
        r2_ref[out_rows, :] = rank2.astype(BF16)
        return c
    lax.fori_loop(0, PEER_HEADS, per_head, 0)


def _gate_prep(sc_t):
    nsc, n = sc_t.shape
    rows = PEER_HEADS * PEER_NKEYS
    spec = pl.BlockSpec((rows, LANES), lambda i: (0, i))
    return pl.pallas_call(
        _gate_kernel,
        grid=(n // LANES,),
        in_specs=[pl.BlockSpec((nsc, LANES), lambda i: (0, i))],
        out_specs=[spec, spec, spec, spec],
        out_shape=[
            jax.ShapeDtypeStruct((rows, n), jnp.uint32),
            jax.ShapeDtypeStruct((rows, n), jnp.uint32),
            jax.ShapeDtypeStruct((rows, n), BF16),
            jax.ShapeDtypeStruct((rows, n), BF16),
        ],
        scratch_shapes=[pltpu.VMEM((2, PEER_TOPK, LANES), F32)],
        compiler_params=pltpu.CompilerParams(
            dimension_semantics=("parallel",), vmem_limit_bytes=VMEM_LIMIT),
        name="gate_prep",
    )(sc_t)


def _peer_kernel(h2_ref, x1_ref, al_ref, ll_ref, be_ref, r2_ref, u_ref, vt_ref, gf_ref, o_ref,
                 pre_scr, a_scr, acc_scr):
    e = pl.program_id(1)
    n_e = pl.num_programs(1)
    rows_per_tile = TE_PEER // PEER_NKEYS
    n_lane_chunks = TT_PEER // PEER_LANE_CHUNK
    packed = (PEER_NKEYS // BF16_ROWS, BF16_ROWS, PEER_LANE_CHUNK)

    @pl.when(e == 0)
    def _init():
        acc_scr[...] = jnp.zeros_like(acc_scr)

    pre_scr[...] = jnp.dot(u_ref[...], h2_ref[...], preferred_element_type=F32)

    for lc in range(n_lane_chunks):
        ls = slice(lc * PEER_LANE_CHUNK, (lc + 1) * PEER_LANE_CHUNK)
        al_blk, ll_blk = [], []
        for h in range(PEER_HEADS):
            i_rows = pl.ds(pl.multiple_of(h * PEER_NKEYS + e * rows_per_tile, rows_per_tile),
                           rows_per_tile)
            al_blk.append(al_ref[i_rows, ls])
            ll_blk.append(ll_ref[i_rows, ls])
        for il in range(rows_per_tile):
            rs = slice(il * PEER_NKEYS, (il + 1) * PEER_NKEYS)
            g = jnp.zeros(packed, BF16)
            for h in range(PEER_HEADS):
                hs = slice(h * PEER_NKEYS, (h + 1) * PEER_NKEYS)
                len1 = pltpu.bitcast(jnp.broadcast_to(ll_blk[h][il:il + 1, :],
                                                      (SUBLANES, PEER_LANE_CHUNK)), BF16)
                alpha = pltpu.bitcast(jnp.broadcast_to(al_blk[h][il:il + 1, :],
                                                       (SUBLANES, PEER_LANE_CHUNK)), BF16)
                keep = r2_ref[hs, ls].reshape(packed) < len1[None]
                g = g + jnp.where(keep, be_ref[hs, ls].reshape(packed) * alpha[None], 0.0)
            act = _gelu(pre_scr[rs, ls]).astype(BF16).reshape(packed) * g
            a_scr[rs, ls] = act.reshape(PEER_NKEYS, PEER_LANE_CHUNK)

    acc_scr[...] += jnp.dot(vt_ref[...], a_scr[...], preferred_element_type=F32)

    @pl.when(e == n_e - 1)
    def _epilogue():
        x2 = x1_ref[...] + acc_scr[...].T
        o_ref[...] = _rms(x2, gf_ref[...])


def _peer(h2t, x1, al, ll, be, r2, u, v_t, gf):
    n, d = x1.shape
    rows = PEER_HEADS * PEER_NKEYS
    tt, te = TT_PEER, TE_PEER
    gate_spec = pl.BlockSpec((rows, tt), lambda t, e: (0, t))
    return pl.pallas_call(
        _peer_kernel,
        grid=(n // tt, PEER_EXPERTS // te),
        in_specs=[
            pl.BlockSpec((d, tt), lambda t, e: (0, t)),
            pl.BlockSpec((tt, d), lambda t, e: (t, 0)),
            gate_spec, gate_spec, gate_spec, gate_spec,
            pl.BlockSpec((te, d), lambda t, e: (e, 0)),
            pl.BlockSpec((d, te), lambda t, e: (0, e)),
            pl.BlockSpec((1, d), lambda t, e: (0, 0)),
        ],
        out_specs=pl.BlockSpec((tt, d), lambda t, e: (t, 0)),
        out_shape=jax.ShapeDtypeStruct((n, d), F32),
        scratch_shapes=[
            pltpu.VMEM((te, tt), F32),
            pltpu.VMEM((te, tt), BF16),
            pltpu.VMEM((d, tt), F32),
        ],
        compiler_params=pltpu.CompilerParams(
            dimension_semantics=("parallel", "arbitrary"), vmem_limit_bytes=VMEM_LIMIT),
        name="peer",
    )(h2t, x1, al, ll, be, r2, u, v_t, gf)


def _cast_kernel(x_ref, o_ref, *, transpose):
    x = x_ref[...]
    o_ref[...] = (x.T if transpose else x).astype(BF16)


def _to_bf16(table, transpose):
    n_rows, d = table.shape
    blk = TE_PEER
    if transpose:
        out_shape, out_spec = (d, n_rows), pl.BlockSpec((d, blk), lambda i: (0, i))
    else:
        out_shape, out_spec = (n_rows, d), pl.BlockSpec((blk, d), lambda i: (i, 0))
    return pl.pallas_call(
        functools.partial(_cast_kernel, transpose=transpose),
        grid=(n_rows // blk,),
        in_specs=[pl.BlockSpec((blk, d), lambda i: (i, 0))],
        out_specs=out_spec,
        out_shape=jax.ShapeDtypeStruct(out_shape, BF16),
        compiler_params=pltpu.CompilerParams(
            dimension_semantics=("parallel",), vmem_limit_bytes=VMEM_LIMIT),
        name="table_cast",
    )(table)


def _fourier_weights(w_in_f, w_fourier, seq):
    kc = jnp.arange(FOURIER_GROUP, dtype=jnp.int32)
    ang_c = (2.0 * math.pi / FOURIER_GROUP) * ((kc[:, None] * kc[None, :]) % FOURIER_GROUP).astype(F32)
    scale = 1.0 / math.sqrt(seq * FOURIER_GROUP)
    wf = w_fourier.astype(F32)
    wc = jnp.einsum("ck,gkd->gcd", jnp.cos(ang_c), wf, precision=_HI) * scale
    ws = jnp.einsum("ck,gkd->gcd", jnp.sin(ang_c), wf, precision=_HI) * scale
    w_in_g = w_in_f.astype(F32).reshape(D_MODEL, N_FOURIER_GROUPS, FOURIER_GROUP)
    w_p = jnp.einsum("mgc,gcd->mgd", w_in_g, wc, precision=_HI).reshape(D_MODEL, D_FNO)
    w_q = jnp.einsum("mgc,gcd->mgd", w_in_g, ws, precision=_HI).reshape(D_MODEL, D_FNO)
    return w_p, w_q


def kernel(x, norm1_g, w_in, ssm_a_re, ssm_a_im, ssm_log_step, ssm_b_re, ssm_b_im, ssm_c_re,
           ssm_c_im, ssm_d, w_glu, w_fourier, w_out, norm2_g, w_query, sub_keys, expert_u,
           expert_v, final_g):
    bsz, seq, d = x.shape
    n = bsz * seq
    n_chunks = seq // CHUNK
    assert d == D_MODEL and seq % TM_IN == 0 and seq % TK_DFT == 0 and n % TT_PEER == 0
    assert norm1_g.shape[0] == 1, "single-layer block"
    layer = 0

    w_p, w_q = _fourier_weights(w_in[layer][:, D_SSM:], w_fourier[layer], seq)
    w_all = jnp.concatenate([w_in[layer][:, :D_SSM].astype(F32), w_p, w_q], axis=1).astype(BF16)
    wb, t_mat, wca, wcb, lam16 = _s5_weights(
        ssm_a_re[layer], ssm_a_im[layer], ssm_log_step[layer], ssm_b_re[layer], ssm_b_im[layer],
        ssm_c_re[layer], ssm_c_im[layer], ssm_d[layer])
    wq4 = w_query[layer].astype(F32).reshape(D_MODEL, PEER_HEADS, 2, PEER_HALF)
    wsc_t = jnp.einsum("dhpe,hpke->hpkd", wq4, sub_keys[layer].astype(F32), precision=_HI)
    wsc_t = wsc_t.reshape(PEER_HEADS * 2 * PEER_NKEYS, D_MODEL).astype(BF16)
    u_bf = _to_bf16(expert_u[layer], transpose=False)
    v_t = _to_bf16(expert_v[layer], transpose=True)

    u, p, q = _inproj(x, norm1_g[layer].reshape(1, d).astype(F32), w_all)

    xg = u.reshape(bsz, n_chunks, CHUNK, N_SSM_GROUPS, SSM_GROUP).transpose(3, 1, 0, 2, 4)
    xg = xg.reshape(N_SSM_GROUPS, n_chunks * bsz, CHUNK_W)
    yg = _s5(xg, wb, t_mat, wca, wcb, lam16, n_chunks, bsz)
    y = yg.reshape(N_SSM_GROUPS, n_chunks, bsz, CHUNK, SSM_GROUP).transpose(2, 1, 3, 0, 4)
    y = y.reshape(n, D_SSM)

    f = _seq_dft(p, q, bsz).reshape(n, D_FNO)

    x1, h2t, sc_t = _merge(x.reshape(n, d), y, f, w_glu[layer].astype(BF16),
                          w_out[layer].astype(BF16), norm2_g[layer].reshape(1, d).astype(F32), wsc_t)

    al, ll, be, r2 = _gate_prep(sc_t)
    out = _peer(h2t, x1, al, ll, be, r2, u_bf, v_t, final_g.reshape(1, d).astype(F32))
    return out.reshape(bsz, seq, d)
```

```python
import functools
import math

import jax
import jax.numpy as jnp
from jax import lax
from jax.experimental import pallas as pl
from jax.experimental.pallas import tpu as pltpu

F32 = jnp.float32
BF16 = jnp.bfloat16

D_MODEL = 1024
D_SSM = 512
D_FNO = 512
SSM_GROUP = 16
N_SSM_GROUPS = 32
SSM_STATE = 64
N_FOURIER_GROUPS = 4
FOURIER_GROUP = 128
PEER_HEADS = 8
PEER_NKEYS = 128
PEER_EXPERTS = PEER_NKEYS * PEER_NKEYS
PEER_TOPK = 16
PEER_HALF = 128
RMS_EPS = 1e-6

CHUNK = 16
CHUNK_W = CHUNK * SSM_GROUP
LANES = 128
VMEM_LIMIT = 56 * 1024 * 1024

TM_IN = 512
TK_DFT = 512
TT_PEER = 512
SUBLANES = 8
BF16_ROWS = 16
PEER_LANE_CHUNK = 256
GATE_LANES = 256
TE_PEER = 2 * SUBLANES * PEER_NKEYS

_HI = lax.Precision.HIGHEST


def _rms(x, g):
    return x * lax.rsqrt(jnp.mean(x * x, axis=-1, keepdims=True) + RMS_EPS) * g


def _gelu(x):
    return 0.5 * x * (1.0 + lax.erf(x * math.sqrt(0.5)))


def _inproj_kernel(x_ref, g_ref, w_ref, u3_ref, p_ref, q_ref, zu_scr, *, bsz, ts):
    h = _rms(x_ref[...].reshape(bsz * ts, D_MODEL), g_ref[...])
    z = jnp.dot(h.astype(BF16), w_ref[...], preferred_element_type=F32)
    for b in range(bsz):
        rows = slice(b * ts, (b + 1) * ts)
        p_ref[:, b * D_FNO:(b + 1) * D_FNO] = z[rows, D_SSM:D_SSM + D_FNO].astype(BF16)
        q_ref[:, b * D_FNO:(b + 1) * D_FNO] = z[rows, D_SSM + D_FNO:].astype(BF16)
    n_lt = D_SSM // LANES
    for j in range(n_lt):
        zu_scr[j] = z[:, j * LANES:(j + 1) * LANES]
    for t in range(CHUNK):
        for cl in range(ts // CHUNK):
            rows = [zu_scr[j, pl.ds(cl * CHUNK + t, bsz, stride=ts), :] for j in range(n_lt)]
            u3_ref[t, cl] = jnp.concatenate(rows, axis=1).astype(BF16)


def _inproj(x, g, w_all, ts):
    bsz, seq, d = x.shape
    nw = w_all.shape[1]
    ct = ts // CHUNK
    return pl.pallas_call(
        functools.partial(_inproj_kernel, bsz=bsz, ts=ts),
        grid=(seq // ts,),
        in_specs=[
            pl.BlockSpec((bsz, ts, d), lambda i: (0, i, 0)),
            pl.BlockSpec((1, d), lambda i: (0, 0)),
            pl.BlockSpec((d, nw), lambda i: (0, 0)),
        ],
        out_specs=[
            pl.BlockSpec((CHUNK, ct, bsz, D_SSM), lambda i: (0, i, 0, 0)),
            pl.BlockSpec((ts, bsz * D_FNO), lambda i: (i, 0)),
            pl.BlockSpec((ts, bsz * D_FNO), lambda i: (i, 0)),
        ],
        out_shape=[
            jax.ShapeDtypeStruct((CHUNK, seq // CHUNK, bsz, D_SSM), BF16),
            jax.ShapeDtypeStruct((seq, bsz * D_FNO), BF16),
            jax.ShapeDtypeStruct((seq, bsz * D_FNO), BF16),
        ],
        scratch_shapes=[pltpu.VMEM((D_SSM // LANES, bsz * ts, LANES), F32)],
        compiler_params=pltpu.CompilerParams(
            dimension_semantics=("parallel",), vmem_limit_bytes=VMEM_LIMIT),
        name="inproj",
    )(x, g, w_all)


def _swap_lane_groups(arrs):
    n = LANES // SSM_GROUP
    grp = lax.broadcasted_iota(jnp.int32, arrs[0].shape, 1) // SSM_GROUP
    arrs = list(arrs)
    s = n // 2
    while s:
        low = (grp & s) == 0
        nxt = list(arrs)
        for p0 in range(n):
            if p0 & s:
                continue
            p1 = p0 | s
            nxt[p0] = jnp.where(low, arrs[p0], pltpu.roll(arrs[p1], SSM_GROUP * s, 1))
            nxt[p1] = jnp.where(low, pltpu.roll(arrs[p0], LANES - SSM_GROUP * s, 1), arrs[p1])
        arrs = nxt
        s //= 2
    return arrs


def _to_groups_kernel(u3_ref, xg_ref, *, rows):
    n = LANES // SSM_GROUP
    halves = []
    for th in range(CHUNK // n):
        a = [pltpu.bitcast(u3_ref[th * n + q].reshape(rows, LANES), jnp.uint32) for q in range(n)]
        halves.append(_swap_lane_groups(a))
    for g in range(n):
        xg_ref[g] = pltpu.bitcast(jnp.concatenate([hv[g] for hv in halves], axis=1), BF16)


def _from_groups_kernel(yg_ref, y3_ref, *, cb, bsz):
    n = LANES // SSM_GROUP
    words = [pltpu.bitcast(yg_ref[g], jnp.uint32) for g in range(n)]
    for th in range(CHUNK // n):
        b = _swap_lane_groups([w[:, th * LANES:(th + 1) * LANES] for w in words])
        for tl in range(n):
            y3_ref[th * n + tl] = pltpu.bitcast(b[tl], BF16).reshape(cb, bsz, LANES)


def _regroup_blocks(n_chunks, bsz):
    cb = min(n_chunks, TM_IN // bsz)
    return cb, cb * bsz


def _to_groups(u3):
    _, n_chunks, bsz, _ = u3.shape
    cb, rows = _regroup_blocks(n_chunks, bsz)
    n = LANES // SSM_GROUP
    return pl.pallas_call(
        functools.partial(_to_groups_kernel, rows=rows),
        grid=(n_chunks // cb, D_SSM // LANES),
        in_specs=[pl.BlockSpec((CHUNK, cb, bsz, LANES), lambda c, g: (0, c, 0, g))],
        out_specs=pl.BlockSpec((n, rows, CHUNK_W), lambda c, g: (g, c, 0)),
        out_shape=jax.ShapeDtypeStruct((N_SSM_GROUPS, n_chunks * bsz, CHUNK_W), BF16),
        compiler_params=pltpu.CompilerParams(
            dimension_semantics=("parallel", "parallel"), vmem_limit_bytes=VMEM_LIMIT),
        name="s5_to_groups",
    )(u3)


def _from_groups(yg, n_chunks, bsz):
    cb, rows = _regroup_blocks(n_chunks, bsz)
    n = LANES // SSM_GROUP
    return pl.pallas_call(
        functools.partial(_from_groups_kernel, cb=cb, bsz=bsz),
        grid=(n_chunks // cb, D_SSM // LANES),
        in_specs=[pl.BlockSpec((n, rows, CHUNK_W), lambda c, g: (g, c, 0))],
        out_specs=pl.BlockSpec((CHUNK, cb, bsz, LANES), lambda c, g: (0, c, 0, g)),
        out_shape=jax.ShapeDtypeStruct((CHUNK, n_chunks, bsz, D_SSM), BF16),
        compiler_params=pltpu.CompilerParams(
            dimension_semantics=("parallel", "parallel"), vmem_limit_bytes=VMEM_LIMIT),
        name="s5_from_groups",
    )(yg)


def _s5_kernel(x_ref, wb_ref, t_ref, wca_ref, wcb_ref, lam_ref, y_ref, s_scr, ha_scr, hb_scr,
               *, n_chunks, bsz):
    rows = n_chunks * bsz
    rblk = min(rows, 512)
    def summarize(r, c):
        rs = pl.ds(pl.multiple_of(r * rblk, rblk), rblk)
        s_scr[rs, :] = jnp.dot(x_ref[rs, :], wb_ref[...], preferred_element_type=F32)
        return c
    lax.fori_loop(0, rows // rblk, summarize, 0)

    lr = lam_ref[0:1, :]
    li = lam_ref[1:2, :]
    is_fwd = lax.broadcasted_iota(jnp.int32, (bsz, LANES), 1) < SSM_STATE

    def step(k, carry):
        hre, him = carry
        rf = pl.ds(pl.multiple_of(k * bsz, bsz), bsz)
        rb = pl.ds(pl.multiple_of((n_chunks - 1 - k) * bsz, bsz), bsz)
        ha_scr[rf, 0:LANES] = hre
        ha_scr[rf, LANES:2 * LANES] = him
        hb_scr[rb, 0:LANES] = hre
        hb_scr[rb, LANES:2 * LANES] = him
        sre = jnp.where(is_fwd, s_scr[rf, 0:LANES], s_scr[rb, 0:LANES])
        sim = jnp.where(is_fwd, s_scr[rf, LANES:2 * LANES], s_scr[rb, LANES:2 * LANES])
        nre = lr * hre - li * him + sre
        nim = lr * him + li * hre + sim
        return nre, nim
    zero = jnp.zeros((bsz, LANES), F32)
    lax.fori_loop(0, n_chunks, step, (zero, zero), unroll=4)

    def emit(r, c):
        rs = pl.ds(pl.multiple_of(r * rblk, rblk), rblk)
        y = jnp.dot(x_ref[rs, :], t_ref[...], preferred_element_type=F32)
        y = y + jnp.dot(ha_scr[rs, :].astype(BF16), wca_ref[...], preferred_element_type=F32)
        y = y + jnp.dot(hb_scr[rs, :].astype(BF16), wcb_ref[...], preferred_element_type=F32)
        y_ref[rs, :] = y.astype(BF16)
        return c
    lax.fori_loop(0, rows // rblk, emit, 0)


def _s5(xg, wb, tt, wca, wcb, lam16, n_chunks, bsz):
    ng, rows, w = xg.shape
    wspec = pl.BlockSpec((None, w, w), lambda g: (g, 0, 0))
    return pl.pallas_call(
        functools.partial(_s5_kernel, n_chunks=n_chunks, bsz=bsz),
        grid=(ng,),
        in_specs=[
            pl.BlockSpec((None, rows, w), lambda g: (g, 0, 0)),
            wspec, wspec, wspec, wspec,
            pl.BlockSpec((None, 2, LANES), lambda g: (g, 0, 0)),
        ],
        out_specs=pl.BlockSpec((None, rows, w), lambda g: (g, 0, 0)),
        out_shape=jax.ShapeDtypeStruct((ng, rows, w), BF16),
        scratch_shapes=[
            pltpu.VMEM((rows, w), F32),
            pltpu.VMEM((rows, w), F32),
            pltpu.VMEM((rows, w), F32),
        ],
        compiler_params=pltpu.CompilerParams(
            dimension_semantics=("parallel",), vmem_limit_bytes=VMEM_LIMIT),
        name="s5",
    )(xg, wb, tt, wca, wcb, lam16)


def _s5_weights(a_re, a_im, log_step, b_re, b_im, c_re, c_im, d_skip):
    g_n, p_n, cg = N_SSM_GROUPS, SSM_STATE, SSM_GROUP
    tau = jnp.arange(CHUNK + 1, dtype=F32)

    def direction(d):
        lam = lax.complex(a_re[d].astype(F32), a_im[d].astype(F32))
        step = jnp.exp(log_step[d].astype(F32))[:, None]
        lam_step = lam * step
        lam_bar = jnp.exp(lam_step)
        b_bar = ((lam_bar - 1.0) / lam)[..., None] * lax.complex(b_re[d].astype(F32), b_im[d].astype(F32))
        c = lax.complex(c_re[d].astype(F32), c_im[d].astype(F32))
        lam_pow = jnp.exp(lam_step[None] * tau[:, None, None])
        kern = jnp.einsum("gop,tgp,gpi->tgoi", c, lam_pow[:CHUNK], b_bar, precision=_HI).real
        return lam_pow, b_bar, c, kern

    lp_f, bb_f, c_f, k_f = direction(0)
    lp_b, bb_b, c_b, k_b = direction(1)

    t_idx = jnp.arange(CHUNK)
    lag = t_idx[None, :] - t_idx[:, None]
    kf = jnp.where((lag >= 0)[:, :, None, None, None], k_f[jnp.clip(lag, 0, CHUNK - 1)], 0.0)
    kb = jnp.where((lag <= 0)[:, :, None, None, None], k_b[jnp.clip(-lag, 0, CHUNK - 1)], 0.0)
    eye_t = jnp.eye(CHUNK, dtype=F32)[:, :, None, None, None]
    eye_c = jnp.eye(cg, dtype=F32)[None, None, None]
    dmat = eye_t * eye_c * d_skip.astype(F32)[None, None, :, :, None]
    tot = kf + kb + dmat
    t_mat = tot.transpose(2, 0, 4, 1, 3).reshape(g_n, CHUNK_W, CHUNK_W)

    wb_f = jnp.einsum("tgp,gpi->gtip", lp_f[:CHUNK][::-1], bb_f).reshape(g_n, CHUNK_W, p_n)
    wb_b = jnp.einsum("tgp,gpi->gtip", lp_b[:CHUNK], bb_b).reshape(g_n, CHUNK_W, p_n)
    wb = jnp.concatenate([wb_f.real, wb_b.real, wb_f.imag, wb_b.imag], axis=-1)

    wc_f = jnp.einsum("gop,tgp->gpto", c_f, lp_f[1:CHUNK + 1]).reshape(g_n, p_n, CHUNK_W)
    wc_b = jnp.einsum("gop,tgp->gpto", c_b, lp_b[1:CHUNK + 1][::-1]).reshape(g_n, p_n, CHUNK_W)
    zeros = jnp.zeros_like(wc_f.real)
    wca = jnp.concatenate([wc_f.real, zeros, -wc_f.imag, zeros], axis=1)
    wcb = jnp.concatenate([zeros, wc_b.real, zeros, -wc_b.imag], axis=1)

    lam16 = jnp.stack([
        jnp.concatenate([lp_f[CHUNK].real, lp_b[CHUNK].real], axis=-1),
        jnp.concatenate([lp_f[CHUNK].imag, lp_b[CHUNK].imag], axis=-1)], axis=1)
    return wb.astype(BF16), t_mat.astype(BF16), wca.astype(BF16), wcb.astype(BF16), lam16


def _dft_kernel(p_ref, q_ref, o_ref, cd_scr, sd_scr, ct_scr, st_scr, *, seq):
    i = pl.program_id(0)
    b = pl.program_id(1)
    rblk = BF16_ROWS
    n_idx = lax.broadcasted_iota(jnp.int32, (rblk, seq), 1)
    w0 = 2.0 * math.pi / seq

    @pl.when((i == 0) & (b == 0))
    def _base_tables():
        def rows(r, c):
            rs = pl.ds(pl.multiple_of(r * rblk, rblk), rblk)
            dk = lax.broadcasted_iota(jnp.int32, (rblk, seq), 0) + r * rblk
            ang = w0 * ((n_idx * dk) & (seq - 1)).astype(F32)
            cd_scr[rs, :] = jnp.cos(ang)
            sd_scr[rs, :] = jnp.sin(ang)
            return c
        lax.fori_loop(0, TK_DFT // rblk, rows, 0)

    @pl.when(b == 0)
    def _tile_tables():
        ang0 = w0 * ((n_idx[0:1, :] * (i * TK_DFT)) & (seq - 1)).astype(F32)
        cb = jnp.cos(ang0)
        sb = jnp.sin(ang0)

        def rows(r, c):
            rs = pl.ds(pl.multiple_of(r * rblk, rblk), rblk)
            cd = cd_scr[rs, :]
            sd = sd_scr[rs, :]
            ct_scr[rs, :] = (cb * cd - sb * sd).astype(BF16)
            st_scr[rs, :] = (-(sb * cd + cb * sd)).astype(BF16)
            return c
        lax.fori_loop(0, TK_DFT // rblk, rows, 0)

    acc = jnp.dot(ct_scr[...], p_ref[...], preferred_element_type=F32)
    acc = acc + jnp.dot(st_scr[...], q_ref[...], preferred_element_type=F32)
    o_ref[...] = acc.astype(BF16)


def _seq_dft(p, q, bsz):
    seq = p.shape[0]
    assert seq & (seq - 1) == 0, "angle reduction uses a power-of-two sequence length"
    return pl.pallas_call(
        functools.partial(_dft_kernel, seq=seq),
        grid=(seq // TK_DFT, bsz),
        in_specs=[
            pl.BlockSpec((seq, D_FNO), lambda i, b: (0, b)),
            pl.BlockSpec((seq, D_FNO), lambda i, b: (0, b)),
        ],
        out_specs=pl.BlockSpec((None, TK_DFT, D_FNO), lambda i, b: (b, i, 0)),
        out_shape=jax.ShapeDtypeStruct((bsz, seq, D_FNO), BF16),
        scratch_shapes=[
            pltpu.VMEM((TK_DFT, seq), F32),
            pltpu.VMEM((TK_DFT, seq), F32),
            pltpu.VMEM((TK_DFT, seq), BF16),
            pltpu.VMEM((TK_DFT, seq), BF16),
        ],
        compiler_params=pltpu.CompilerParams(
            dimension_semantics=("arbitrary", "arbitrary"), vmem_limit_bytes=VMEM_LIMIT),
        name="seq_dft",
    )(p, q)


def _merge_kernel(x_ref, y3_ref, f_ref, wglu_ref, wout_ref, g2_ref, wsc_ref,
                  x1_ref, h2_ref, sc_ref, y_scr, *, bsz, ts):
    n_lt = D_SSM // LANES
    for t in range(CHUNK):
        for cl in range(ts // CHUNK):
            yv = y3_ref[t, cl].astype(F32)
            for j in range(n_lt):
                y_scr[j, pl.ds(cl * CHUNK + t, bsz, stride=ts), :] = yv[:, j * LANES:(j + 1) * LANES]
    y = jnp.concatenate([y_scr[j] for j in range(n_lt)], axis=1)
    yg = _gelu(y)
    gate = jax.nn.sigmoid(jnp.dot(yg.astype(BF16), wglu_ref[...], preferred_element_type=F32))
    ys = (yg * gate).astype(BF16)
    mix = jnp.dot(ys, wout_ref[0:D_SSM, :], preferred_element_type=F32)
    mix = mix + jnp.dot(f_ref[...].reshape(bsz * ts, D_FNO), wout_ref[D_SSM:, :],
                        preferred_element_type=F32)
    x1 = x_ref[...].reshape(bsz * ts, D_MODEL) + mix
    x1_ref[...] = x1.reshape(bsz, ts, D_MODEL)
    h2t = _rms(x1, g2_ref[...]).T.astype(BF16)
    h2_ref[...] = h2t
    sc_ref[...] = jnp.dot(wsc_ref[...], h2t, preferred_element_type=F32)


def _merge(x, y3, f, wglu, wout, g2, wsc_t, ts):
    bsz, seq, d = x.shape
    n = bsz * seq
    nsc = wsc_t.shape[0]
    tm = bsz * ts
    ct = ts // CHUNK
    return pl.pallas_call(
        functools.partial(_merge_kernel, bsz=bsz, ts=ts),
        grid=(seq // ts,),
        in_specs=[
            pl.BlockSpec((bsz, ts, d), lambda i: (0, i, 0)),
            pl.BlockSpec((CHUNK, ct, bsz, D_SSM), lambda i: (0, i, 0, 0)),
            pl.BlockSpec((bsz, ts, D_FNO), lambda i: (0, i, 0)),
            pl.BlockSpec((D_SSM, D_SSM), lambda i: (0, 0)),
            pl.BlockSpec((d, d), lambda i: (0, 0)),
            pl.BlockSpec((1, d), lambda i: (0, 0)),
            pl.BlockSpec((nsc, d), lambda i: (0, 0)),
        ],
        out_specs=[
            pl.BlockSpec((bsz, ts, d), lambda i: (0, i, 0)),
            pl.BlockSpec((d, tm), lambda i: (0, i)),
            pl.BlockSpec((nsc, tm), lambda i: (0, i)),
        ],
        out_shape=[
            jax.ShapeDtypeStruct((bsz, seq, d), F32),
            jax.ShapeDtypeStruct((d, n), BF16),
            jax.ShapeDtypeStruct((nsc, n), F32),
        ],
        scratch_shapes=[pltpu.VMEM((D_SSM // LANES, tm, LANES), F32)],
        compiler_params=pltpu.CompilerParams(
            dimension_semantics=("parallel",), vmem_limit_bytes=VMEM_LIMIT),
        name="merge",
    )(x, y3, f, wglu, wout, g2, wsc_t)


_PARTIAL_ROWS = ((2, 5), (3, 4), (4, 3), (5, 2), (6, 2), (7, 2))


def _dup_bf16(v):
    hi = pltpu.bitcast(v.astype(BF16).astype(F32), jnp.uint32)
    return hi | (hi >> 16)


def _gate_kernel(sc_ref, al_ref, ll_ref, be_ref, r2_ref, v_scr):
    neg = -jnp.inf
    riota = lax.broadcasted_iota(jnp.int32, (SUBLANES, GATE_LANES), 0)

    def per_head(h, c):
        s1 = sc_ref[pl.ds(pl.multiple_of(2 * h * PEER_NKEYS, PEER_NKEYS), PEER_NKEYS), :]
        s2 = sc_ref[pl.ds(pl.multiple_of((2 * h + 1) * PEER_NKEYS, PEER_NKEYS), PEER_NKEYS), :]
        out_rows = pl.ds(pl.multiple_of(h * PEER_NKEYS, PEER_NKEYS), PEER_NKEYS)
        s = s1
        for k in range(PEER_TOPK):
            m = jnp.max(s, axis=0, keepdims=True)
            v_scr[0, k:k + 1, :] = m
            s = jnp.where(s == m, neg, s)
        s = s2
        rank2 = jnp.full((PEER_NKEYS, GATE_LANES), float(PEER_TOPK), F32)
        for k in range(PEER_TOPK):
            m = jnp.max(s, axis=0, keepdims=True)
            v_scr[1, k:k + 1, :] = m
            hit = s == m
            rank2 = jnp.where(hit, float(k), rank2)
            s = jnp.where(hit, neg, s)
        v1a = v_scr[0, 0:8, :]
        v1b = v_scr[0, 8:16, :]
        v2a = v_scr[1, 0:8, :]
        v2b = v_scr[1, 8:16, :]
        top1 = v_scr[0, 0:1, :]
        top2 = v_scr[1, 0:1, :]
        blocks = [top1 + v2a, top1 + v2b, v_scr[0, 1:2, :] + v2a]
        for a, nb in _PARTIAL_ROWS:
            blocks.append(jnp.where(riota < nb, v_scr[0, a:a + 1, :] + v2a, neg))
        blocks.append(v1b + top2)
        cand = jnp.concatenate(blocks, axis=0)
        mtop = top1 + top2
        z = jnp.zeros((1, GATE_LANES), F32)
        tau = mtop
        for k in range(PEER_TOPK):
            m = jnp.max(cand, axis=0, keepdims=True)
            live = m > neg
            z = z + jnp.where(live, jnp.exp(m - mtop), 0.0)
            tau = jnp.where(live, m, tau)
            cand = jnp.where(cand == m, neg, cand)
        len1 = jnp.zeros((PEER_NKEYS, GATE_LANES), F32)
        for b in range(PEER_TOPK):
            len1 = jnp.where((s1 + v_scr[1, b:b + 1, :]) >= tau, float(b + 1), len1)
        al_ref[out_rows, :] = _dup_bf16(jnp.exp(s1 - top1) * (1.0 / z))
        ll_ref[out_rows, :] = _dup_bf16(len1)
        be_ref[out_rows, :] = jnp.exp(s2 - top2).astype(BF16)
        r2_ref[out_rows, :] = rank2.astype(BF16)
        return c
    lax.fori_loop(0, PEER_HEADS, per_head, 0)


def _gate_prep(sc_t):
    nsc, n = sc_t.shape
    rows = PEER_HEADS * PEER_NKEYS
    spec = pl.BlockSpec((rows, GATE_LANES), lambda i: (0, i))
    return pl.pallas_call(
        _gate_kernel,
        grid=(n // GATE_LANES,),
        in_specs=[pl.BlockSpec((nsc, GATE_LANES), lambda i: (0, i))],
        out_specs=[spec, spec, spec, spec],
        out_shape=[
            jax.ShapeDtypeStruct((rows, n), jnp.uint32),
            jax.ShapeDtypeStruct((rows, n), jnp.uint32),
            jax.ShapeDtypeStruct((rows, n), BF16),
            jax.ShapeDtypeStruct((rows, n), BF16),
        ],
        scratch_shapes=[pltpu.VMEM((2, PEER_TOPK, GATE_LANES), F32)],
        compiler_params=pltpu.CompilerParams(
            dimension_semantics=("parallel",), vmem_limit_bytes=VMEM_LIMIT),
        name="gate_prep",
    )(sc_t)


def _peer_kernel(h2_ref, x1_ref, al_ref, ll_ref, be_ref, r2_ref, u_ref, vt_ref, gf_ref, o_ref,
                 pre_scr, a_scr, acc_scr):
    e = pl.program_id(1)
    n_e = pl.num_programs(1)
    rows_per_tile = TE_PEER // PEER_NKEYS
    n_lane_chunks = TT_PEER // PEER_LANE_CHUNK
    packed = (PEER_NKEYS // BF16_ROWS, BF16_ROWS, PEER_LANE_CHUNK)

    @pl.when(e == 0)
    def _init():
        acc_scr[...] = jnp.zeros_like(acc_scr)

    pre_scr[...] = jnp.dot(u_ref[...], h2_ref[...], preferred_element_type=F32)

    for lc in range(n_lane_chunks):
        ls = slice(lc * PEER_LANE_CHUNK, (lc + 1) * PEER_LANE_CHUNK)
        al_blk, ll_blk = [], []
        for h in range(PEER_HEADS):
            i_rows = pl.ds(pl.multiple_of(h * PEER_NKEYS + e * rows_per_tile, rows_per_tile),
                           rows_per_tile)
            al_blk.append(al_ref[i_rows, ls])
            ll_blk.append(ll_ref[i_rows, ls])
        for il in range(rows_per_tile):
            rs = slice(il * PEER_NKEYS, (il + 1) * PEER_NKEYS)
            g = jnp.zeros(packed, BF16)
            for h in range(PEER_HEADS):
                hs = slice(h * PEER_NKEYS, (h + 1) * PEER_NKEYS)
                len1 = pltpu.bitcast(jnp.broadcast_to(ll_blk[h][il:il + 1, :],
                                                      (SUBLANES, PEER_LANE_CHUNK)), BF16)
                alpha = pltpu.bitcast(jnp.broadcast_to(al_blk[h][il:il + 1, :],
                                                       (SUBLANES, PEER_LANE_CHUNK)), BF16)
                keep = r2_ref[hs, ls].reshape(packed) < len1[None]
                g = g + jnp.where(keep, be_ref[hs, ls].reshape(packed) * alpha[None], 0.0)
            act = _gelu(pre_scr[rs, ls]).astype(BF16).reshape(packed) * g
            a_scr[rs, ls] = act.reshape(PEER_NKEYS, PEER_LANE_CHUNK)

    acc_scr[...] += jnp.dot(vt_ref[...], a_scr[...], preferred_element_type=F32)

    @pl.when(e == n_e - 1)
    def _epilogue():
        x2 = x1_ref[...].reshape(TT_PEER, D_MODEL) + acc_scr[...].T
        o_ref[...] = _rms(x2, gf_ref[...]).reshape(o_ref.shape)


def _peer(h2t, x1, al, ll, be, r2, u, v_t, gf, ts):
    bsz, seq, d = x1.shape
    rows = PEER_HEADS * PEER_NKEYS
    tt, te = TT_PEER, TE_PEER
    assert tt == bsz * ts, "PEER token tile = one (all batches x ts positions) tile"
    gate_spec = pl.BlockSpec((rows, tt), lambda t, e: (0, t))
    return pl.pallas_call(
        _peer_kernel,
        grid=(seq // ts, PEER_EXPERTS // te),
        in_specs=[
            pl.BlockSpec((d, tt), lambda t, e: (0, t)),
            pl.BlockSpec((bsz, ts, d), lambda t, e: (0, t, 0)),
            gate_spec, gate_spec, gate_spec, gate_spec,
            pl.BlockSpec((te, d), lambda t, e: (e, 0)),
            pl.BlockSpec((d, te), lambda t, e: (0, e)),
            pl.BlockSpec((1, d), lambda t, e: (0, 0)),
        ],
        out_specs=pl.BlockSpec((bsz, ts, d), lambda t, e: (0, t, 0)),
        out_shape=jax.ShapeDtypeStruct((bsz, seq, d), F32),
        scratch_shapes=[
            pltpu.VMEM((te, tt), F32),
            pltpu.VMEM((te, tt), BF16),
            pltpu.VMEM((d, tt), F32),
        ],
        compiler_params=pltpu.CompilerParams(
            dimension_semantics=("parallel", "arbitrary"), vmem_limit_bytes=VMEM_LIMIT),
        name="peer",
    )(h2t, x1, al, ll, be, r2, u, v_t, gf)


def _cast_kernel(x_ref, o_ref, *, transpose):
    x = x_ref[...]
    o_ref[...] = (x.T if transpose else x).astype(BF16)


def _to_bf16(table, transpose):
    n_rows, d = table.shape
    blk = TE_PEER
    if transpose:
        out_shape, out_spec = (d, n_rows), pl.BlockSpec((d, blk), lambda i: (0, i))
    else:
        out_shape, out_spec = (n_rows, d), pl.BlockSpec((blk, d), lambda i: (i, 0))
    return pl.pallas_call(
        functools.partial(_cast_kernel, transpose=transpose),
        grid=(n_rows // blk,),
        in_specs=[pl.BlockSpec((blk, d), lambda i: (i, 0))],
        out_specs=out_spec,
        out_shape=jax.ShapeDtypeStruct(out_shape, BF16),
        compiler_params=pltpu.CompilerParams(
            dimension_semantics=("parallel",), vmem_limit_bytes=VMEM_LIMIT),
        name="table_cast",
    )(table)


def _fourier_weights(w_in_f, w_fourier, seq):
    kc = jnp.arange(FOURIER_GROUP, dtype=jnp.int32)
    ang_c = (2.0 * math.pi / FOURIER_GROUP) * ((kc[:, None] * kc[None, :]) % FOURIER_GROUP).astype(F32)
    scale = 1.0 / math.sqrt(seq * FOURIER_GROUP)
    wf = w_fourier.astype(F32)
    wc = jnp.einsum("ck,gkd->gcd", jnp.cos(ang_c), wf, precision=_HI) * scale
    ws = jnp.einsum("ck,gkd->gcd", jnp.sin(ang_c), wf, precision=_HI) * scale
    w_in_g = w_in_f.astype(F32).reshape(D_MODEL, N_FOURIER_GROUPS, FOURIER_GROUP)
    w_p = jnp.einsum("mgc,gcd->mgd", w_in_g, wc, precision=_HI).reshape(D_MODEL, D_FNO)
    w_q = jnp.einsum("mgc,gcd->mgd", w_in_g, ws, precision=_HI).reshape(D_MODEL, D_FNO)
    return w_p, w_q


def kernel(x, norm1_g, w_in, ssm_a_re, ssm_a_im, ssm_log_step, ssm_b_re, ssm_b_im, ssm_c_re,
           ssm_c_im, ssm_d, w_glu, w_fourier, w_out, norm2_g, w_query, sub_keys, expert_u,
           expert_v, final_g):
    bsz, seq, d = x.shape
    n = bsz * seq
    n_chunks = seq // CHUNK
    ts = TM_IN // bsz
    assert d == D_MODEL and seq % TK_DFT == 0 and TM_IN % bsz == 0 and n % TT_PEER == 0
    assert ts % CHUNK == 0 and ts % BF16_ROWS == 0 and seq % ts == 0
    assert norm1_g.shape[0] == 1, "single-layer block"
    layer = 0

    w_p, w_q = _fourier_weights(w_in[layer][:, D_SSM:], w_fourier[layer], seq)
    w_all = jnp.concatenate([w_in[layer][:, :D_SSM].astype(F32), w_p, w_q], axis=1).astype(BF16)
    wb, t_mat, wca, wcb, lam16 = _s5_weights(
        ssm_a_re[layer], ssm_a_im[layer], ssm_log_step[layer], ssm_b_re[layer], ssm_b_im[layer],
        ssm_c_re[layer], ssm_c_im[layer], ssm_d[layer])
    wq4 = w_query[layer].astype(F32).reshape(D_MODEL, PEER_HEADS, 2, PEER_HALF)
    wsc_t = jnp.einsum("dhpe,hpke->hpkd", wq4, sub_keys[layer].astype(F32), precision=_HI)
    wsc_t = wsc_t.reshape(PEER_HEADS * 2 * PEER_NKEYS, D_MODEL).astype(BF16)
    u_bf = _to_bf16(expert_u[layer], transpose=False)
    v_t = _to_bf16(expert_v[layer], transpose=True)

    u3, p, q = _inproj(x, norm1_g[layer].reshape(1, d).astype(F32), w_all, ts)

    yg = _s5(_to_groups(u3), wb, t_mat, wca, wcb, lam16, n_chunks, bsz)
    y3 = _from_groups(yg, n_chunks, bsz)

    f = _seq_dft(p, q, bsz)

    x1, h2t, sc_t = _merge(x, y3, f, w_glu[layer].astype(BF16), w_out[layer].astype(BF16),
                           norm2_g[layer].reshape(1, d).astype(F32), wsc_t, ts)

    al, ll, be, r2 = _gate_prep(sc_t)
    return _peer(h2t, x1, al, ll, be, r2, u_bf, v_t, final_g.reshape(1, d).astype(F32), ts)
```

```python
import functools
import math

import jax
import jax.numpy as jnp
from jax import lax
from jax.experimental import pallas as pl
from jax.experimental.pallas import tpu as pltpu

F32 = jnp.float32
BF16 = jnp.bfloat16

D_MODEL = 1024
D_SSM = 512
D_FNO = 512
SSM_GROUP = 16
N_SSM_GROUPS = 32
SSM_STATE = 64
N_FOURIER_GROUPS = 4
FOURIER_GROUP = 128
PEER_HEADS = 8
PEER_NKEYS = 128
PEER_EXPERTS = PEER_NKEYS * PEER_NKEYS
PEER_TOPK = 16
PEER_HALF = 128
RMS_EPS = 1e-6

CHUNK = 16
CHUNK_W = CHUNK * SSM_GROUP
LANES = 128
VMEM_LIMIT = 56 * 1024 * 1024

TM_IN = 512
TK_DFT = 512
TT_PEER = 512
SUBLANES = 8
BF16_ROWS = 16
PEER_LANE_CHUNK = 256
GATE_LANES = 256
TE_PEER = 2 * SUBLANES * PEER_NKEYS

_HI = lax.Precision.HIGHEST


def _rms(x, g):
    return x * lax.rsqrt(jnp.mean(x * x, axis=-1, keepdims=True) + RMS_EPS) * g


def _gelu(x):
    return 0.5 * x * (1.0 + lax.erf(x * math.sqrt(0.5)))


def _inproj_kernel(x_ref, g_ref, w_ref, u3_ref, p_ref, q_ref, zu_scr, *, bsz, ts):
    h = _rms(x_ref[...].reshape(bsz * ts, D_MODEL), g_ref[...])
    z = jnp.dot(h.astype(BF16), w_ref[...], preferred_element_type=F32)
    for b in range(bsz):
        rows = slice(b * ts, (b + 1) * ts)
        p_ref[:, b * D_FNO:(b + 1) * D_FNO] = z[rows, D_SSM:D_SSM + D_FNO].astype(BF16)
        q_ref[:, b * D_FNO:(b + 1) * D_FNO] = z[rows, D_SSM + D_FNO:].astype(BF16)
    n_lt = D_SSM // LANES
    for j in range(n_lt):
        zu_scr[j] = z[:, j * LANES:(j + 1) * LANES]
    for t in range(CHUNK):
        for cl in range(ts // CHUNK):
            rows = [zu_scr[j, pl.ds(cl * CHUNK + t, bsz, stride=ts), :] for j in range(n_lt)]
            u3_ref[t, cl] = jnp.concatenate(rows, axis=1).astype(BF16)


def _inproj(x, g, w_all, ts):
    bsz, seq, d = x.shape
    nw = w_all.shape[1]
    ct = ts // CHUNK
    return pl.pallas_call(
        functools.partial(_inproj_kernel, bsz=bsz, ts=ts),
        grid=(seq // ts,),
        in_specs=[
            pl.BlockSpec((bsz, ts, d), lambda i: (0, i, 0)),
            pl.BlockSpec((1, d), lambda i: (0, 0)),
            pl.BlockSpec((d, nw), lambda i: (0, 0)),
        ],
        out_specs=[
            pl.BlockSpec((CHUNK, ct, bsz, D_SSM), lambda i: (0, i, 0, 0)),
            pl.BlockSpec((ts, bsz * D_FNO), lambda i: (i, 0)),
            pl.BlockSpec((ts, bsz * D_FNO), lambda i: (i, 0)),
        ],
        out_shape=[
            jax.ShapeDtypeStruct((CHUNK, seq // CHUNK, bsz, D_SSM), BF16),
            jax.ShapeDtypeStruct((seq, bsz * D_FNO), BF16),
            jax.ShapeDtypeStruct((seq, bsz * D_FNO), BF16),
        ],
        scratch_shapes=[pltpu.VMEM((D_SSM // LANES, bsz * ts, LANES), F32)],
        compiler_params=pltpu.CompilerParams(
            dimension_semantics=("parallel",), vmem_limit_bytes=VMEM_LIMIT),
        name="inproj",
    )(x, g, w_all)


def _swap_lane_groups(arrs):
    n = LANES // SSM_GROUP
    grp = lax.broadcasted_iota(jnp.int32, arrs[0].shape, 1) // SSM_GROUP
    arrs = list(arrs)
    s = n // 2
    while s:
        low = (grp & s) == 0
        nxt = list(arrs)
        for p0 in range(n):
            if p0 & s:
                continue
            p1 = p0 | s
            nxt[p0] = jnp.where(low, arrs[p0], pltpu.roll(arrs[p1], SSM_GROUP * s, 1))
            nxt[p1] = jnp.where(low, pltpu.roll(arrs[p0], LANES - SSM_GROUP * s, 1), arrs[p1])
        arrs = nxt
        s //= 2
    return arrs


def _to_groups_kernel(u3_ref, xg_ref, *, rows):
    n = LANES // SSM_GROUP
    halves = []
    for th in range(CHUNK // n):
        a = [pltpu.bitcast(u3_ref[th * n + q].reshape(rows, LANES), jnp.uint32) for q in range(n)]
        halves.append(_swap_lane_groups(a))
    for g in range(n):
        xg_ref[g] = pltpu.bitcast(jnp.concatenate([hv[g] for hv in halves], axis=1), BF16)


def _from_groups_kernel(yg_ref, y3_ref, *, cb, bsz):
    n = LANES // SSM_GROUP
    words = [pltpu.bitcast(yg_ref[g], jnp.uint32) for g in range(n)]
    for th in range(CHUNK // n):
        b = _swap_lane_groups([w[:, th * LANES:(th + 1) * LANES] for w in words])
        for tl in range(n):
            y3_ref[th * n + tl] = pltpu.bitcast(b[tl], BF16).reshape(cb, bsz, LANES)


def _regroup_blocks(n_chunks, bsz):
    cb = min(n_chunks, TM_IN // bsz)
    return cb, cb * bsz


def _to_groups(u3):
    _, n_chunks, bsz, _ = u3.shape
    cb, rows = _regroup_blocks(n_chunks, bsz)
    n = LANES // SSM_GROUP
    return pl.pallas_call(
        functools.partial(_to_groups_kernel, rows=rows),
        grid=(n_chunks // cb, D_SSM // LANES),
        in_specs=[pl.BlockSpec((CHUNK, cb, bsz, LANES), lambda c, g: (0, c, 0, g))],
        out_specs=pl.BlockSpec((n, rows, CHUNK_W), lambda c, g: (g, c, 0)),
        out_shape=jax.ShapeDtypeStruct((N_SSM_GROUPS, n_chunks * bsz, CHUNK_W), BF16),
        compiler_params=pltpu.CompilerParams(
            dimension_semantics=("parallel", "parallel"), vmem_limit_bytes=VMEM_LIMIT),
        name="s5_to_groups",
    )(u3)


def _from_groups(yg, n_chunks, bsz):
    cb, rows = _regroup_blocks(n_chunks, bsz)
    n = LANES // SSM_GROUP
    return pl.pallas_call(
        functools.partial(_from_groups_kernel, cb=cb, bsz=bsz),
        grid=(n_chunks // cb, D_SSM // LANES),
        in_specs=[pl.BlockSpec((n, rows, CHUNK_W), lambda c, g: (g, c, 0))],
        out_specs=pl.BlockSpec((CHUNK, cb, bsz, LANES), lambda c, g: (0, c, 0, g)),
        out_shape=jax.ShapeDtypeStruct((CHUNK, n_chunks, bsz, D_SSM), BF16),
        compiler_params=pltpu.CompilerParams(
            dimension_semantics=("parallel", "parallel"), vmem_limit_bytes=VMEM_LIMIT),
        name="s5_from_groups",
    )(yg)


def _s5_kernel(x_ref, wb_ref, t_ref, wca_ref, wcb_ref, lam_ref, y_ref, s_scr, ha_scr, hb_scr,
               *, n_chunks, bsz):
    rows = n_chunks * bsz
    rblk = min(rows, 512)
    def summarize(r, c):
        rs = pl.ds(pl.multiple_of(r * rblk, rblk), rblk)
        s_scr[rs, :] = jnp.dot(x_ref[rs, :], wb_ref[...], preferred_element_type=F32)
        return c
    lax.fori_loop(0, rows // rblk, summarize, 0)

    lr = lam_ref[0:1, :]
    li = lam_ref[1:2, :]
    is_fwd = lax.broadcasted_iota(jnp.int32, (bsz, LANES), 1) < SSM_STATE

    def step(k, carry):
        hre, him = carry
        rf = pl.ds(pl.multiple_of(k * bsz, bsz), bsz)
        rb = pl.ds(pl.multiple_of((n_chunks - 1 - k) * bsz, bsz), bsz)
        ha_scr[rf, 0:LANES] = hre
        ha_scr[rf, LANES:2 * LANES] = him
        hb_scr[rb, 0:LANES] = hre
        hb_scr[rb, LANES:2 * LANES] = him
        sre = jnp.where(is_fwd, s_scr[rf, 0:LANES], s_scr[rb, 0:LANES])
        sim = jnp.where(is_fwd, s_scr[rf, LANES:2 * LANES], s_scr[rb, LANES:2 * LANES])
        nre = lr * hre - li * him + sre
        nim = lr * him + li * hre + sim
        return nre, nim
    zero = jnp.zeros((bsz, LANES), F32)
    lax.fori_loop(0, n_chunks, step, (zero, zero), unroll=4)

    def emit(r, c):
        rs = pl.ds(pl.multiple_of(r * rblk, rblk), rblk)
        y = jnp.dot(x_ref[rs, :], t_ref[...], preferred_element_type=F32)
        y = y + jnp.dot(ha_scr[rs, :].astype(BF16), wca_ref[...], preferred_element_type=F32)
        y = y + jnp.dot(hb_scr[rs, :].astype(BF16), wcb_ref[...], preferred_element_type=F32)
        y_ref[rs, :] = y.astype(BF16)
        return c
    lax.fori_loop(0, rows // rblk, emit, 0)


def _s5(xg, wb, tt, wca, wcb, lam16, n_chunks, bsz):
    ng, rows, w = xg.shape
    wspec = pl.BlockSpec((None, w, w), lambda g: (g, 0, 0))
    return pl.pallas_call(
        functools.partial(_s5_kernel, n_chunks=n_chunks, bsz=bsz),
        grid=(ng,),
        in_specs=[
            pl.BlockSpec((None, rows, w), lambda g: (g, 0, 0)),
            wspec, wspec, wspec, wspec,
            pl.BlockSpec((None, 2, LANES), lambda g: (g, 0, 0)),
        ],
        out_specs=pl.BlockSpec((None, rows, w), lambda g: (g, 0, 0)),
        out_shape=jax.ShapeDtypeStruct((ng, rows, w), BF16),
        scratch_shapes=[
            pltpu.VMEM((rows, w), F32),
            pltpu.VMEM((rows, w), F32),
            pltpu.VMEM((rows, w), F32),
        ],
        compiler_params=pltpu.CompilerParams(
            dimension_semantics=("parallel",), vmem_limit_bytes=VMEM_LIMIT),
        name="s5",
    )(xg, wb, tt, wca, wcb, lam16)


def _s5_weights(a_re, a_im, log_step, b_re, b_im, c_re, c_im, d_skip):
    g_n, p_n, cg = N_SSM_GROUPS, SSM_STATE, SSM_GROUP
    tau = jnp.arange(CHUNK + 1, dtype=F32)

    def direction(d):
        lam = lax.complex(a_re[d].astype(F32), a_im[d].astype(F32))
        step = jnp.exp(log_step[d].astype(F32))[:, None]
        lam_step = lam * step
        lam_bar = jnp.exp(lam_step)
        b_bar = ((lam_bar - 1.0) / lam)[..., None] * lax.complex(b_re[d].astype(F32), b_im[d].astype(F32))
        c = lax.complex(c_re[d].astype(F32), c_im[d].astype(F32))
        lam_pow = jnp.exp(lam_step[None] * tau[:, None, None])
        kern = jnp.einsum("gop,tgp,gpi->tgoi", c, lam_pow[:CHUNK], b_bar, precision=_HI).real
        return lam_pow, b_bar, c, kern

    lp_f, bb_f, c_f, k_f = direction(0)
    lp_b, bb_b, c_b, k_b = direction(1)

    t_idx = jnp.arange(CHUNK)
    lag = t_idx[None, :] - t_idx[:, None]
    kf = jnp.where((lag >= 0)[:, :, None, None, None], k_f[jnp.clip(lag, 0, CHUNK - 1)], 0.0)
    kb = jnp.where((lag <= 0)[:, :, None, None, None], k_b[jnp.clip(-lag, 0, CHUNK - 1)], 0.0)
    eye_t = jnp.eye(CHUNK, dtype=F32)[:, :, None, None, None]
    eye_c = jnp.eye(cg, dtype=F32)[None, None, None]
    dmat = eye_t * eye_c * d_skip.astype(F32)[None, None, :, :, None]
    tot = kf + kb + dmat
    t_mat = tot.transpose(2, 0, 4, 1, 3).reshape(g_n, CHUNK_W, CHUNK_W)

    wb_f = jnp.einsum("tgp,gpi->gtip", lp_f[:CHUNK][::-1], bb_f).reshape(g_n, CHUNK_W, p_n)
    wb_b = jnp.einsum("tgp,gpi->gtip", lp_b[:CHUNK], bb_b).reshape(g_n, CHUNK_W, p_n)
    wb = jnp.concatenate([wb_f.real, wb_b.real, wb_f.imag, wb_b.imag], axis=-1)

    wc_f = jnp.einsum("gop,tgp->gpto", c_f, lp_f[1:CHUNK + 1]).reshape(g_n, p_n, CHUNK_W)
    wc_b = jnp.einsum("gop,tgp->gpto", c_b, lp_b[1:CHUNK + 1][::-1]).reshape(g_n, p_n, CHUNK_W)
    zeros = jnp.zeros_like(wc_f.real)
    wca = jnp.concatenate([wc_f.real, zeros, -wc_f.imag, zeros], axis=1)
    wcb = jnp.concatenate([zeros, wc_b.real, zeros, -wc_b.imag], axis=1)

    lam16 = jnp.stack([
        jnp.concatenate([lp_f[CHUNK].real, lp_b[CHUNK].real], axis=-1),
        jnp.concatenate([lp_f[CHUNK].imag, lp_b[CHUNK].imag], axis=-1)], axis=1)
    return wb.astype(BF16), t_mat.astype(BF16), wca.astype(BF16), wcb.astype(BF16), lam16


def _dft_kernel(p_ref, q_ref, flip_ref, lo_ref, hi_ref, cd_scr, sd_scr, ct_scr, st_scr, *, seq):
    i = pl.program_id(0)
    b = pl.program_id(1)
    rblk = BF16_ROWS
    n_rows = TK_DFT + BF16_ROWS
    n_idx = lax.broadcasted_iota(jnp.int32, (rblk, seq), 1)
    w0 = 2.0 * math.pi / seq

    @pl.when((i == 0) & (b == 0))
    def _base_tables():
        def rows(r, c):
            rs = pl.ds(pl.multiple_of(r * rblk, rblk), rblk)
            dk = lax.broadcasted_iota(jnp.int32, (rblk, seq), 0) + r * rblk
            ang = w0 * ((n_idx * dk) & (seq - 1)).astype(F32)
            cd_scr[rs, :] = jnp.cos(ang)
            sd_scr[rs, :] = jnp.sin(ang)
            return c
        lax.fori_loop(0, n_rows // rblk, rows, 0)

    @pl.when(b == 0)
    def _tile_tables():
        ang0 = w0 * ((n_idx[0:1, :] * (i * TK_DFT)) & (seq - 1)).astype(F32)
        cb = jnp.cos(ang0)
        sb = jnp.sin(ang0)

        def rows(r, c):
            rs = pl.ds(pl.multiple_of(r * rblk, rblk), rblk)
            cd = cd_scr[rs, :]
            sd = sd_scr[rs, :]
            ct_scr[rs, :] = (cb * cd - sb * sd).astype(BF16)
            st_scr[rs, :] = (sb * cd + cb * sd).astype(BF16)
            return c
        lax.fori_loop(0, n_rows // rblk, rows, 0)

    ck = jnp.dot(ct_scr[...], p_ref[...], preferred_element_type=F32)
    sk = jnp.dot(st_scr[...], q_ref[...], preferred_element_type=F32)
    lo_ref[...] = (ck - sk)[0:TK_DFT, :].astype(BF16)
    hi_ref[...] = jnp.dot(flip_ref[...], (ck + sk).astype(BF16),
                          preferred_element_type=F32).astype(BF16)


def _seq_dft(p, q, bsz):
    seq = p.shape[0]
    assert seq & (seq - 1) == 0, "angle reduction uses a power-of-two sequence length"
    n_half = seq // (2 * TK_DFT)
    n_rows = TK_DFT + BF16_ROWS
    flip = (jnp.arange(TK_DFT)[:, None] + jnp.arange(n_rows)[None, :] == TK_DFT).astype(BF16)
    half = jax.ShapeDtypeStruct((bsz, seq // 2, D_FNO), BF16)
    return pl.pallas_call(
        functools.partial(_dft_kernel, seq=seq),
        grid=(n_half, bsz),
        in_specs=[
            pl.BlockSpec((seq, D_FNO), lambda i, b: (0, b)),
            pl.BlockSpec((seq, D_FNO), lambda i, b: (0, b)),
            pl.BlockSpec((TK_DFT, n_rows), lambda i, b: (0, 0)),
        ],
        out_specs=[
            pl.BlockSpec((None, TK_DFT, D_FNO), lambda i, b: (b, i, 0)),
            pl.BlockSpec((None, TK_DFT, D_FNO), lambda i, b: (b, n_half - 1 - i, 0)),
        ],
        out_shape=[half, half],
        scratch_shapes=[
            pltpu.VMEM((n_rows, seq), F32),
            pltpu.VMEM((n_rows, seq), F32),
            pltpu.VMEM((n_rows, seq), BF16),
            pltpu.VMEM((n_rows, seq), BF16),
        ],
        compiler_params=pltpu.CompilerParams(
            dimension_semantics=("arbitrary", "arbitrary"), vmem_limit_bytes=VMEM_LIMIT),
        name="seq_dft",
    )(p, q, flip)


def _merge_kernel(x_ref, y3_ref, f_ref, wglu_ref, wout_ref, g2_ref, wsc_ref,
                  x1_ref, h2_ref, sc_ref, y_scr, *, bsz, ts):
    n_lt = D_SSM // LANES
    for t in range(CHUNK):
        for cl in range(ts // CHUNK):
            yv = y3_ref[t, cl].astype(F32)
            for j in range(n_lt):
                y_scr[j, pl.ds(cl * CHUNK + t, bsz, stride=ts), :] = yv[:, j * LANES:(j + 1) * LANES]
    y = jnp.concatenate([y_scr[j] for j in range(n_lt)], axis=1)
    yg = _gelu(y)
    gate = jax.nn.sigmoid(jnp.dot(yg.astype(BF16), wglu_ref[...], preferred_element_type=F32))
    ys = (yg * gate).astype(BF16)
    mix = jnp.dot(ys, wout_ref[0:D_SSM, :], preferred_element_type=F32)
    mix = mix + jnp.dot(f_ref[...].reshape(bsz * ts, D_FNO), wout_ref[D_SSM:, :],
                        preferred_element_type=F32)
    x1 = x_ref[...].reshape(bsz * ts, D_MODEL) + mix
    x1_ref[...] = x1.reshape(bsz, ts, D_MODEL)
    h2t = _rms(x1, g2_ref[...]).T.astype(BF16)
    h2_ref[...] = h2t
    sc_ref[...] = jnp.dot(wsc_ref[...], h2t, preferred_element_type=F32)


def _merge(x, y3, f, wglu, wout, g2, wsc_t, ts):
    bsz, seq, d = x.shape
    n = bsz * seq
    nsc = wsc_t.shape[0]
    tm = bsz * ts
    ct = ts // CHUNK
    return pl.pallas_call(
        functools.partial(_merge_kernel, bsz=bsz, ts=ts),
        grid=(seq // ts,),
        in_specs=[
            pl.BlockSpec((bsz, ts, d), lambda i: (0, i, 0)),
            pl.BlockSpec((CHUNK, ct, bsz, D_SSM), lambda i: (0, i, 0, 0)),
            pl.BlockSpec((bsz, ts, D_FNO), lambda i: (0, i, 0)),
            pl.BlockSpec((D_SSM, D_SSM), lambda i: (0, 0)),
            pl.BlockSpec((d, d), lambda i: (0, 0)),
            pl.BlockSpec((1, d), lambda i: (0, 0)),
            pl.BlockSpec((nsc, d), lambda i: (0, 0)),
        ],
        out_specs=[
            pl.BlockSpec((bsz, ts, d), lambda i: (0, i, 0)),
            pl.BlockSpec((d, tm), lambda i: (0, i)),
            pl.BlockSpec((nsc, tm), lambda i: (0, i)),
        ],
        out_shape=[
            jax.ShapeDtypeStruct((bsz, seq, d), F32),
            jax.ShapeDtypeStruct((d, n), BF16),
            jax.ShapeDtypeStruct((nsc, n), F32),
        ],
        scratch_shapes=[pltpu.VMEM((D_SSM // LANES, tm, LANES), F32)],
        compiler_params=pltpu.CompilerParams(
            dimension_semantics=("parallel",), vmem_limit_bytes=VMEM_LIMIT),
        name="merge",
    )(x, y3, f, wglu, wout, g2, wsc_t)


_PARTIAL_ROWS = ((2, 5), (3, 4), (4, 3), (5, 2), (6, 2), (7, 2))


def _dup_bf16(v):
    hi = pltpu.bitcast(v.astype(BF16).astype(F32), jnp.uint32)
    return hi | (hi >> 16)


def _gate_kernel(sc_ref, al_ref, ll_ref, be_ref, r2_ref, v_scr):
    neg = -jnp.inf
    riota = lax.broadcasted_iota(jnp.int32, (SUBLANES, GATE_LANES), 0)

    def per_head(h, c):
        s1 = sc_ref[pl.ds(pl.multiple_of(2 * h * PEER_NKEYS, PEER_NKEYS), PEER_NKEYS), :]
        s2 = sc_ref[pl.ds(pl.multiple_of((2 * h + 1) * PEER_NKEYS, PEER_NKEYS), PEER_NKEYS), :]
        out_rows = pl.ds(pl.multiple_of(h * PEER_NKEYS, PEER_NKEYS), PEER_NKEYS)
        s = s1
        for k in range(PEER_TOPK):
            m = jnp.max(s, axis=0, keepdims=True)
            v_scr[0, k:k + 1, :] = m
            s = jnp.where(s == m, neg, s)
        s = s2
        rank2 = jnp.full((PEER_NKEYS, GATE_LANES), float(PEER_TOPK), F32)
        for k in range(PEER_TOPK):
            m = jnp.max(s, axis=0, keepdims=True)
            v_scr[1, k:k + 1, :] = m
            hit = s == m
            rank2 = jnp.where(hit, float(k), rank2)
            s = jnp.where(hit, neg, s)
        v1a = v_scr[0, 0:8, :]
        v1b = v_scr[0, 8:16, :]
        v2a = v_scr[1, 0:8, :]
        v2b = v_scr[1, 8:16, :]
        top1 = v_scr[0, 0:1, :]
        top2 = v_scr[1, 0:1, :]
        blocks = [top1 + v2a, top1 + v2b, v_scr[0, 1:2, :] + v2a]
        for a, nb in _PARTIAL_ROWS:
            blocks.append(jnp.where(riota < nb, v_scr[0, a:a + 1, :] + v2a, neg))
        blocks.append(v1b + top2)
        cand = jnp.concatenate(blocks, axis=0)
        mtop = top1 + top2
        z = jnp.zeros((1, GATE_LANES), F32)
        tau = mtop
        for k in range(PEER_TOPK):
            m = jnp.max(cand, axis=0, keepdims=True)
            live = m > neg
            z = z + jnp.where(live, jnp.exp(m - mtop), 0.0)
            tau = jnp.where(live, m, tau)
            cand = jnp.where(cand == m, neg, cand)
        len1 = jnp.zeros((PEER_NKEYS, GATE_LANES), F32)
        for b in range(PEER_TOPK):
            len1 = jnp.where((s1 + v_scr[1, b:b + 1, :]) >= tau, float(b + 1), len1)
        al_ref[out_rows, :] = _dup_bf16(jnp.exp(s1 - top1) * (1.0 / z))
        ll_ref[out_rows, :] = _dup_bf16(len1)
        be_ref[out_rows, :] = jnp.exp(s2 - top2).astype(BF16)
        r2_ref[out_rows, :] = rank2.astype(BF16)
        return c
    lax.fori_loop(0, PEER_HEADS, per_head, 0)


def _gate_prep(sc_t):
    nsc, n = sc_t.shape
    rows = PEER_HEADS * PEER_NKEYS
    spec = pl.BlockSpec((rows, GATE_LANES), lambda i: (0, i))
    return pl.pallas_call(
        _gate_kernel,
        grid=(n // GATE_LANES,),
        in_specs=[pl.BlockSpec((nsc, GATE_LANES), lambda i: (0, i))],
        out_specs=[spec, spec, spec, spec],
        out_shape=[
            jax.ShapeDtypeStruct((rows, n), jnp.uint32),
            jax.ShapeDtypeStruct((rows, n), jnp.uint32),
            jax.ShapeDtypeStruct((rows, n), BF16),
            jax.ShapeDtypeStruct((rows, n), BF16),
        ],
        scratch_shapes=[pltpu.VMEM((2, PEER_TOPK, GATE_LANES), F32)],
        compiler_params=pltpu.CompilerParams(
            dimension_semantics=("parallel",), vmem_limit_bytes=VMEM_LIMIT),
        name="gate_prep",
    )(sc_t)


def _peer_kernel(h2_ref, x1_ref, al_ref, ll_ref, be_ref, r2_ref, u_ref, vt_ref, gf_ref, o_ref,
                 pre_scr, a_scr, acc_scr):
    e = pl.program_id(1)
    n_e = pl.num_programs(1)
    rows_per_tile = TE_PEER // PEER_NKEYS
    n_lane_chunks = TT_PEER // PEER_LANE_CHUNK
    packed = (PEER_NKEYS // BF16_ROWS, BF16_ROWS, PEER_LANE_CHUNK)

    @pl.when(e == 0)
    def _init():
        acc_scr[...] = jnp.zeros_like(acc_scr)

    pre_scr[...] = jnp.dot(u_ref[...], h2_ref[...], preferred_element_type=F32)

    for lc in range(n_lane_chunks):
        ls = slice(lc * PEER_LANE_CHUNK, (lc + 1) * PEER_LANE_CHUNK)
        al_blk, ll_blk = [], []
        for h in range(PEER_HEADS):
            i_rows = pl.ds(pl.multiple_of(h * PEER_NKEYS + e * rows_per_tile, rows_per_tile),
                           rows_per_tile)
            al_blk.append(al_ref[i_rows, ls])
            ll_blk.append(ll_ref[i_rows, ls])
        for il in range(rows_per_tile):
            rs = slice(il * PEER_NKEYS, (il + 1) * PEER_NKEYS)
            g = jnp.zeros(packed, BF16)
            for h in range(PEER_HEADS):
                hs = slice(h * PEER_NKEYS, (h + 1) * PEER_NKEYS)
                len1 = pltpu.bitcast(jnp.broadcast_to(ll_blk[h][il:il + 1, :],
                                                      (SUBLANES, PEER_LANE_CHUNK)), BF16)
                alpha = pltpu.bitcast(jnp.broadcast_to(al_blk[h][il:il + 1, :],
                                                       (SUBLANES, PEER_LANE_CHUNK)), BF16)
                keep = r2_ref[hs, ls].reshape(packed) < len1[None]
                g = g + jnp.where(keep, be_ref[hs, ls].reshape(packed) * alpha[None], 0.0)
            act = _gelu(pre_scr[rs, ls]).astype(BF16).reshape(packed) * g
            a_scr[rs, ls] = act.reshape(PEER_NKEYS, PEER_LANE_CHUNK)

    acc_scr[...] += jnp.dot(vt_ref[...], a_scr[...], preferred_element_type=F32)

    @pl.when(e == n_e - 1)
    def _epilogue():
        x2 = x1_ref[...].reshape(TT_PEER, D_MODEL) + acc_scr[...].T
        o_ref[...] = _rms(x2, gf_ref[...]).reshape(o_ref.shape)


def _peer(h2t, x1, al, ll, be, r2, u, v_t, gf, ts):
    bsz, seq, d = x1.shape
    rows = PEER_HEADS * PEER_NKEYS
    tt, te = TT_PEER, TE_PEER
    assert tt == bsz * ts, "PEER token tile = one (all batches x ts positions) tile"
    gate_spec = pl.BlockSpec((rows, tt), lambda t, e: (0, t))
    return pl.pallas_call(
        _peer_kernel,
        grid=(seq // ts, PEER_EXPERTS // te),
        in_specs=[
            pl.BlockSpec((d, tt), lambda t, e: (0, t)),
            pl.BlockSpec((bsz, ts, d), lambda t, e: (0, t, 0)),
            gate_spec, gate_spec, gate_spec, gate_spec,
            pl.BlockSpec((te, d), lambda t, e: (e, 0)),
            pl.BlockSpec((d, te), lambda t, e: (0, e)),
            pl.BlockSpec((1, d), lambda t, e: (0, 0)),
        ],
        out_specs=pl.BlockSpec((bsz, ts, d), lambda t, e: (0, t, 0)),
        out_shape=jax.ShapeDtypeStruct((bsz, seq, d), F32),
        scratch_shapes=[
            pltpu.VMEM((te, tt), F32),
            pltpu.VMEM((te, tt), BF16),
            pltpu.VMEM((d, tt), F32),
        ],
        compiler_params=pltpu.CompilerParams(
            dimension_semantics=("parallel", "arbitrary"), vmem_limit_bytes=VMEM_LIMIT),
        name="peer",
    )(h2t, x1, al, ll, be, r2, u, v_t, gf)


def _cast_kernel(x_ref, o_ref, *, transpose):
    x = x_ref[...]
    o_ref[...] = (x.T if transpose else x).astype(BF16)


def _to_bf16(table, transpose):
    n_rows, d = table.shape
    blk = TE_PEER
    if transpose:
        out_shape, out_spec = (d, n_rows), pl.BlockSpec((d, blk), lambda i: (0, i))
    else:
        out_shape, out_spec = (n_rows, d), pl.BlockSpec((blk, d), lambda i: (i, 0))
    return pl.pallas_call(
        functools.partial(_cast_kernel, transpose=transpose),
        grid=(n_rows // blk,),
        in_specs=[pl.BlockSpec((blk, d), lambda i: (i, 0))],
        out_specs=out_spec,
        out_shape=jax.ShapeDtypeStruct(out_shape, BF16),
        compiler_params=pltpu.CompilerParams(
            dimension_semantics=("parallel",), vmem_limit_bytes=VMEM_LIMIT),
        name="table_cast",
    )(table)


def _fourier_weights(w_in_f, w_fourier, seq):
    kc = jnp.arange(FOURIER_GROUP, dtype=jnp.int32)
    ang_c = (2.0 * math.pi / FOURIER_GROUP) * ((kc[:, None] * kc[None, :]) % FOURIER_GROUP).astype(F32)
    scale = 1.0 / math.sqrt(seq * FOURIER_GROUP)
    wf = w_fourier.astype(F32)
    wc = jnp.einsum("ck,gkd->gcd", jnp.cos(ang_c), wf, precision=_HI) * scale
    ws = jnp.einsum("ck,gkd->gcd", jnp.sin(ang_c), wf, precision=_HI) * scale
    w_in_g = w_in_f.astype(F32).reshape(D_MODEL, N_FOURIER_GROUPS, FOURIER_GROUP)
    w_p = jnp.einsum("mgc,gcd->mgd", w_in_g, wc, precision=_HI).reshape(D_MODEL, D_FNO)
    w_q = jnp.einsum("mgc,gcd->mgd", w_in_g, ws, precision=_HI).reshape(D_MODEL, D_FNO)
    return w_p, w_q


def kernel(x, norm1_g, w_in, ssm_a_re, ssm_a_im, ssm_log_step, ssm_b_re, ssm_b_im, ssm_c_re,
           ssm_c_im, ssm_d, w_glu, w_fourier, w_out, norm2_g, w_query, sub_keys, expert_u,
           expert_v, final_g):
    bsz, seq, d = x.shape
    n = bsz * seq
    n_chunks = seq // CHUNK
    ts = TM_IN // bsz
    assert d == D_MODEL and seq % (2 * TK_DFT) == 0 and TM_IN % bsz == 0 and n % TT_PEER == 0
    assert ts % CHUNK == 0 and ts % BF16_ROWS == 0 and seq % ts == 0
    assert norm1_g.shape[0] == 1, "single-layer block"
    layer = 0

    w_p, w_q = _fourier_weights(w_in[layer][:, D_SSM:], w_fourier[layer], seq)
    w_all = jnp.concatenate([w_in[layer][:, :D_SSM].astype(F32), w_p, w_q], axis=1).astype(BF16)
    wb, t_mat, wca, wcb, lam16 = _s5_weights(
        ssm_a_re[layer], ssm_a_im[layer], ssm_log_step[layer], ssm_b_re[layer], ssm_b_im[layer],
        ssm_c_re[layer], ssm_c_im[layer], ssm_d[layer])
    wq4 = w_query[layer].astype(F32).reshape(D_MODEL, PEER_HEADS, 2, PEER_HALF)
    wsc_t = jnp.einsum("dhpe,hpke->hpkd", wq4, sub_keys[layer].astype(F32), precision=_HI)
    wsc_t = wsc_t.reshape(PEER_HEADS * 2 * PEER_NKEYS, D_MODEL).astype(BF16)
    u_bf = _to_bf16(expert_u[layer], transpose=False)
    v_t = _to_bf16(expert_v[layer], transpose=True)

    u3, p, q = _inproj(x, norm1_g[layer].reshape(1, d).astype(F32), w_all, ts)

    yg = _s5(_to_groups(u3), wb, t_mat, wca, wcb, lam16, n_chunks, bsz)
    y3 = _from_groups(yg, n_chunks, bsz)

    f = jnp.concatenate(_seq_dft(p, q, bsz), axis=1)

    x1, h2t, sc_t = _merge(x, y3, f, w_glu[layer].astype(BF16), w_out[layer].astype(BF16),
                           norm2_g[layer].reshape(1, d).astype(F32), wsc_t, ts)

    al, ll, be, r2 = _gate_prep(sc_t)
    return _peer(h2t, x1, al, ll, be, r2, u_bf, v_t, final_g.reshape(1, d).astype(F32), ts)
```

```python
import functools
import math

import jax
import jax.numpy as jnp
from jax import lax
from jax.experimental import pallas as pl
from jax.experimental.pallas import tpu as pltpu

F32 = jnp.float32
BF16 = jnp.bfloat16

D_MODEL = 1024
D_SSM = 512
D_FNO = 512
SSM_GROUP = 16
N_SSM_GROUPS = 32
SSM_STATE = 64
N_FOURIER_GROUPS = 4
FOURIER_GROUP = 128
PEER_HEADS = 8
PEER_NKEYS = 128
PEER_EXPERTS = PEER_NKEYS * PEER_NKEYS
PEER_TOPK = 16
PEER_HALF = 128
RMS_EPS = 1e-6

CHUNK = 16
CHUNK_W = CHUNK * SSM_GROUP
LANES = 128
VMEM_LIMIT = 56 * 1024 * 1024

TM_IN = 512
TK_DFT = 512
TT_PEER = 512
SUBLANES = 8
BF16_ROWS = 16
PEER_LANE_CHUNK = 256
GATE_LANES = 256
TE_PEER = 2 * SUBLANES * PEER_NKEYS

_HI = lax.Precision.HIGHEST


def _rms(x, g):
    return x * lax.rsqrt(jnp.mean(x * x, axis=-1, keepdims=True) + RMS_EPS) * g


def _gelu(x):
    return 0.5 * x * (1.0 + lax.erf(x * math.sqrt(0.5)))


def _inproj_kernel(x_ref, g_ref, w_ref, u3_ref, p_ref, q_ref, zu_scr, *, bsz, ts):
    h = _rms(x_ref[...].reshape(bsz * ts, D_MODEL), g_ref[...])
    z = jnp.dot(h.astype(BF16), w_ref[...], preferred_element_type=F32)
    for b in range(bsz):
        rows = slice(b * ts, (b + 1) * ts)
        p_ref[:, b * D_FNO:(b + 1) * D_FNO] = z[rows, D_SSM:D_SSM + D_FNO].astype(BF16)
        q_ref[:, b * D_FNO:(b + 1) * D_FNO] = z[rows, D_SSM + D_FNO:].astype(BF16)
    n_lt = D_SSM // LANES
    for j in range(n_lt):
        zu_scr[j] = z[:, j * LANES:(j + 1) * LANES]
    for t in range(CHUNK):
        for cl in range(ts // CHUNK):
            rows = [zu_scr[j, pl.ds(cl * CHUNK + t, bsz, stride=ts), :] for j in range(n_lt)]
            u3_ref[t, cl] = jnp.concatenate(rows, axis=1).astype(BF16)


def _inproj(x, g, w_all, ts):
    bsz, seq, d = x.shape
    nw = w_all.shape[1]
    ct = ts // CHUNK
    return pl.pallas_call(
        functools.partial(_inproj_kernel, bsz=bsz, ts=ts),
        grid=(seq // ts,),
        in_specs=[
            pl.BlockSpec((bsz, ts, d), lambda i: (0, i, 0)),
            pl.BlockSpec((1, d), lambda i: (0, 0)),
            pl.BlockSpec((d, nw), lambda i: (0, 0)),
        ],
        out_specs=[
            pl.BlockSpec((CHUNK, ct, bsz, D_SSM), lambda i: (0, i, 0, 0)),
            pl.BlockSpec((ts, bsz * D_FNO), lambda i: (i, 0)),
            pl.BlockSpec((ts, bsz * D_FNO), lambda i: (i, 0)),
        ],
        out_shape=[
            jax.ShapeDtypeStruct((CHUNK, seq // CHUNK, bsz, D_SSM), BF16),
            jax.ShapeDtypeStruct((seq, bsz * D_FNO), BF16),
            jax.ShapeDtypeStruct((seq, bsz * D_FNO), BF16),
        ],
        scratch_shapes=[pltpu.VMEM((D_SSM // LANES, bsz * ts, LANES), F32)],
        compiler_params=pltpu.CompilerParams(
            dimension_semantics=("parallel",), vmem_limit_bytes=VMEM_LIMIT),
        name="inproj",
    )(x, g, w_all)


def _swap_lane_groups(arrs):
    n = LANES // SSM_GROUP
    grp = lax.broadcasted_iota(jnp.int32, arrs[0].shape, 1) // SSM_GROUP
    arrs = list(arrs)
    s = n // 2
    while s:
        low = (grp & s) == 0
        nxt = list(arrs)
        for p0 in range(n):
            if p0 & s:
                continue
            p1 = p0 | s
            nxt[p0] = jnp.where(low, arrs[p0], pltpu.roll(arrs[p1], SSM_GROUP * s, 1))
            nxt[p1] = jnp.where(low, pltpu.roll(arrs[p0], LANES - SSM_GROUP * s, 1), arrs[p1])
        arrs = nxt
        s //= 2
    return arrs


def _to_groups_kernel(u3_ref, xg_ref, *, rows):
    n = LANES // SSM_GROUP
    halves = []
    for th in range(CHUNK // n):
        a = [pltpu.bitcast(u3_ref[th * n + q].reshape(rows, LANES), jnp.uint32) for q in range(n)]
        halves.append(_swap_lane_groups(a))
    for g in range(n):
        xg_ref[g] = pltpu.bitcast(jnp.concatenate([hv[g] for hv in halves], axis=1), BF16)


def _from_groups_kernel(yg_ref, y3_ref, *, cb, bsz):
    n = LANES // SSM_GROUP
    words = [pltpu.bitcast(yg_ref[g], jnp.uint32) for g in range(n)]
    for th in range(CHUNK // n):
        b = _swap_lane_groups([w[:, th * LANES:(th + 1) * LANES] for w in words])
        for tl in range(n):
            y3_ref[th * n + tl] = pltpu.bitcast(b[tl], BF16).reshape(cb, bsz, LANES)


def _regroup_blocks(n_chunks, bsz):
    cb = min(n_chunks, TM_IN // bsz)
    return cb, cb * bsz


def _to_groups(u3):
    _, n_chunks, bsz, _ = u3.shape
    cb, rows = _regroup_blocks(n_chunks, bsz)
    n = LANES // SSM_GROUP
    return pl.pallas_call(
        functools.partial(_to_groups_kernel, rows=rows),
        grid=(n_chunks // cb, D_SSM // LANES),
        in_specs=[pl.BlockSpec((CHUNK, cb, bsz, LANES), lambda c, g: (0, c, 0, g))],
        out_specs=pl.BlockSpec((n, rows, CHUNK_W), lambda c, g: (g, c, 0)),
        out_shape=jax.ShapeDtypeStruct((N_SSM_GROUPS, n_chunks * bsz, CHUNK_W), BF16),
        compiler_params=pltpu.CompilerParams(
            dimension_semantics=("parallel", "parallel"), vmem_limit_bytes=VMEM_LIMIT),
        name="s5_to_groups",
    )(u3)


def _from_groups(yg, n_chunks, bsz):
    cb, rows = _regroup_blocks(n_chunks, bsz)
    n = LANES // SSM_GROUP
    return pl.pallas_call(
        functools.partial(_from_groups_kernel, cb=cb, bsz=bsz),
        grid=(n_chunks // cb, D_SSM // LANES),
        in_specs=[pl.BlockSpec((n, rows, CHUNK_W), lambda c, g: (g, c, 0))],
        out_specs=pl.BlockSpec((CHUNK, cb, bsz, LANES), lambda c, g: (0, c, 0, g)),
        out_shape=jax.ShapeDtypeStruct((CHUNK, n_chunks, bsz, D_SSM), BF16),
        compiler_params=pltpu.CompilerParams(
            dimension_semantics=("parallel", "parallel"), vmem_limit_bytes=VMEM_LIMIT),
        name="s5_from_groups",
    )(yg)


def _s5_kernel(x_ref, wb_ref, t_ref, wca_ref, wcb_ref, lam_ref, y_ref, s_scr, ha_scr, hb_scr,
               *, n_chunks, bsz):
    rows = n_chunks * bsz
    rblk = min(rows, 512)
    def summarize(r, c):
        rs = pl.ds(pl.multiple_of(r * rblk, rblk), rblk)
        s_scr[rs, :] = jnp.dot(x_ref[rs, :], wb_ref[...], preferred_element_type=F32)
        return c
    lax.fori_loop(0, rows // rblk, summarize, 0)

    lr = lam_ref[0:1, :]
    li = lam_ref[1:2, :]
    is_fwd = lax.broadcasted_iota(jnp.int32, (bsz, LANES), 1) < SSM_STATE

    def step(k, carry):
        hre, him = carry
        rf = pl.ds(pl.multiple_of(k * bsz, bsz), bsz)
        rb = pl.ds(pl.multiple_of((n_chunks - 1 - k) * bsz, bsz), bsz)
        ha_scr[rf, 0:LANES] = hre
        ha_scr[rf, LANES:2 * LANES] = him
        hb_scr[rb, 0:LANES] = hre
        hb_scr[rb, LANES:2 * LANES] = him
        sre = jnp.where(is_fwd, s_scr[rf, 0:LANES], s_scr[rb, 0:LANES])
        sim = jnp.where(is_fwd, s_scr[rf, LANES:2 * LANES], s_scr[rb, LANES:2 * LANES])
        nre = lr * hre - li * him + sre
        nim = lr * him + li * hre + sim
        return nre, nim
    zero = jnp.zeros((bsz, LANES), F32)
    lax.fori_loop(0, n_chunks, step, (zero, zero), unroll=4)

    def emit(r, c):
        rs = pl.ds(pl.multiple_of(r * rblk, rblk), rblk)
        y = jnp.dot(x_ref[rs, :], t_ref[...], preferred_element_type=F32)
        y = y + jnp.dot(ha_scr[rs, :].astype(BF16), wca_ref[...], preferred_element_type=F32)
        y = y + jnp.dot(hb_scr[rs, :].astype(BF16), wcb_ref[...], preferred_element_type=F32)
        y_ref[rs, :] = y.astype(BF16)
        return c
    lax.fori_loop(0, rows // rblk, emit, 0)


def _s5(xg, wb, tt, wca, wcb, lam16, n_chunks, bsz):
    ng, rows, w = xg.shape
    wspec = pl.BlockSpec((None, w, w), lambda g: (g, 0, 0))
    return pl.pallas_call(
        functools.partial(_s5_kernel, n_chunks=n_chunks, bsz=bsz),
        grid=(ng,),
        in_specs=[
            pl.BlockSpec((None, rows, w), lambda g: (g, 0, 0)),
            wspec, wspec, wspec, wspec,
            pl.BlockSpec((None, 2, LANES), lambda g: (g, 0, 0)),
        ],
        out_specs=pl.BlockSpec((None, rows, w), lambda g: (g, 0, 0)),
        out_shape=jax.ShapeDtypeStruct((ng, rows, w), BF16),
        scratch_shapes=[
            pltpu.VMEM((rows, w), F32),
            pltpu.VMEM((rows, w), F32),
            pltpu.VMEM((rows, w), F32),
        ],
        compiler_params=pltpu.CompilerParams(
            dimension_semantics=("parallel",), vmem_limit_bytes=VMEM_LIMIT),
        name="s5",
    )(xg, wb, tt, wca, wcb, lam16)


def _s5_weights(a_re, a_im, log_step, b_re, b_im, c_re, c_im, d_skip):
    g_n, p_n, cg = N_SSM_GROUPS, SSM_STATE, SSM_GROUP
    tau = jnp.arange(CHUNK + 1, dtype=F32)

    def direction(d):
        lam = lax.complex(a_re[d].astype(F32), a_im[d].astype(F32))
        step = jnp.exp(log_step[d].astype(F32))[:, None]
        lam_step = lam * step
        lam_bar = jnp.exp(lam_step)
        b_bar = ((lam_bar - 1.0) / lam)[..., None] * lax.complex(b_re[d].astype(F32), b_im[d].astype(F32))
        c = lax.complex(c_re[d].astype(F32), c_im[d].astype(F32))
        lam_pow = jnp.exp(lam_step[None] * tau[:, None, None])
        kern = jnp.einsum("gop,tgp,gpi->tgoi", c, lam_pow[:CHUNK], b_bar, precision=_HI).real
        return lam_pow, b_bar, c, kern

    lp_f, bb_f, c_f, k_f = direction(0)
    lp_b, bb_b, c_b, k_b = direction(1)

    t_idx = jnp.arange(CHUNK)
    lag = t_idx[None, :] - t_idx[:, None]
    kf = jnp.where((lag >= 0)[:, :, None, None, None], k_f[jnp.clip(lag, 0, CHUNK - 1)], 0.0)
    kb = jnp.where((lag <= 0)[:, :, None, None, None], k_b[jnp.clip(-lag, 0, CHUNK - 1)], 0.0)
    eye_t = jnp.eye(CHUNK, dtype=F32)[:, :, None, None, None]
    eye_c = jnp.eye(cg, dtype=F32)[None, None, None]
    dmat = eye_t * eye_c * d_skip.astype(F32)[None, None, :, :, None]
    tot = kf + kb + dmat
    t_mat = tot.transpose(2, 0, 4, 1, 3).reshape(g_n, CHUNK_W, CHUNK_W)

    wb_f = jnp.einsum("tgp,gpi->gtip", lp_f[:CHUNK][::-1], bb_f).reshape(g_n, CHUNK_W, p_n)
    wb_b = jnp.einsum("tgp,gpi->gtip", lp_b[:CHUNK], bb_b).reshape(g_n, CHUNK_W, p_n)
    wb = jnp.concatenate([wb_f.real, wb_b.real, wb_f.imag, wb_b.imag], axis=-1)

    wc_f = jnp.einsum("gop,tgp->gpto", c_f, lp_f[1:CHUNK + 1]).reshape(g_n, p_n, CHUNK_W)
    wc_b = jnp.einsum("gop,tgp->gpto", c_b, lp_b[1:CHUNK + 1][::-1]).reshape(g_n, p_n, CHUNK_W)
    zeros = jnp.zeros_like(wc_f.real)
    wca = jnp.concatenate([wc_f.real, zeros, -wc_f.imag, zeros], axis=1)
    wcb = jnp.concatenate([zeros, wc_b.real, zeros, -wc_b.imag], axis=1)

    lam16 = jnp.stack([
        jnp.concatenate([lp_f[CHUNK].real, lp_b[CHUNK].real], axis=-1),
        jnp.concatenate([lp_f[CHUNK].imag, lp_b[CHUNK].imag], axis=-1)], axis=1)
    return wb.astype(BF16), t_mat.astype(BF16), wca.astype(BF16), wcb.astype(BF16), lam16


def _dft_kernel(p_ref, q_ref, flip_ref, lo_ref, hi_ref, cd_scr, sd_scr, ct_scr, st_scr, *, seq):
    i = pl.program_id(0)
    b = pl.program_id(1)
    rblk = BF16_ROWS
    n_rows = TK_DFT + BF16_ROWS
    n_idx = lax.broadcasted_iota(jnp.int32, (rblk, seq), 1)
    w0 = 2.0 * math.pi / seq

    @pl.when((i == 0) & (b == 0))
    def _base_tables():
        def rows(r, c):
            rs = pl.ds(pl.multiple_of(r * rblk, rblk), rblk)
            dk = lax.broadcasted_iota(jnp.int32, (rblk, seq), 0) + r * rblk
            ang = w0 * ((n_idx * dk) & (seq - 1)).astype(F32)
            cd_scr[rs, :] = jnp.cos(ang)
            sd_scr[rs, :] = jnp.sin(ang)
            return c
        lax.fori_loop(0, n_rows // rblk, rows, 0)

    @pl.when(b == 0)
    def _tile_tables():
        ang0 = w0 * ((n_idx[0:1, :] * (i * TK_DFT)) & (seq - 1)).astype(F32)
        cb = jnp.cos(ang0)
        sb = jnp.sin(ang0)

        def rows(r, c):
            rs = pl.ds(pl.multiple_of(r * rblk, rblk), rblk)
            cd = cd_scr[rs, :]
            sd = sd_scr[rs, :]
            ct_scr[rs, :] = (cb * cd - sb * sd).astype(BF16)
            st_scr[rs, :] = (sb * cd + cb * sd).astype(BF16)
            return c
        lax.fori_loop(0, n_rows // rblk, rows, 0)

    ck = jnp.dot(ct_scr[...], p_ref[...], preferred_element_type=F32)
    sk = jnp.dot(st_scr[...], q_ref[...], preferred_element_type=F32)
    lo_ref[...] = (ck - sk)[0:TK_DFT, :].astype(BF16)
    hi_ref[...] = jnp.dot(flip_ref[...], (ck + sk).astype(BF16),
                          preferred_element_type=F32).astype(BF16)


def _seq_dft(p, q, bsz):
    seq = p.shape[0]
    assert seq & (seq - 1) == 0, "angle reduction uses a power-of-two sequence length"
    n_half = seq // (2 * TK_DFT)
    n_rows = TK_DFT + BF16_ROWS
    flip = (jnp.arange(TK_DFT)[:, None] + jnp.arange(n_rows)[None, :] == TK_DFT).astype(BF16)
    half = jax.ShapeDtypeStruct((bsz, seq // 2, D_FNO), BF16)
    return pl.pallas_call(
        functools.partial(_dft_kernel, seq=seq),
        grid=(n_half, bsz),
        in_specs=[
            pl.BlockSpec((seq, D_FNO), lambda i, b: (0, b)),
            pl.BlockSpec((seq, D_FNO), lambda i, b: (0, b)),
            pl.BlockSpec((TK_DFT, n_rows), lambda i, b: (0, 0)),
        ],
        out_specs=[
            pl.BlockSpec((None, TK_DFT, D_FNO), lambda i, b: (b, i, 0)),
            pl.BlockSpec((None, TK_DFT, D_FNO), lambda i, b: (b, n_half - 1 - i, 0)),
        ],
        out_shape=[half, half],
        scratch_shapes=[
            pltpu.VMEM((n_rows, seq), F32),
            pltpu.VMEM((n_rows, seq), F32),
            pltpu.VMEM((n_rows, seq), BF16),
            pltpu.VMEM((n_rows, seq), BF16),
        ],
        compiler_params=pltpu.CompilerParams(
            dimension_semantics=("arbitrary", "arbitrary"), vmem_limit_bytes=VMEM_LIMIT),
        name="seq_dft",
    )(p, q, flip)


def _merge_kernel(x_ref, y3_ref, f_ref, wglu_ref, wout_ref, g2_ref, wsc_ref,
                  x1_ref, h2_ref, sc_ref, y_scr, *, bsz, ts):
    n_lt = D_SSM // LANES
    for t in range(CHUNK):
        for cl in range(ts // CHUNK):
            yv = y3_ref[t, cl].astype(F32)
            for j in range(n_lt):
                y_scr[j, pl.ds(cl * CHUNK + t, bsz, stride=ts), :] = yv[:, j * LANES:(j + 1) * LANES]
    y = jnp.concatenate([y_scr[j] for j in range(n_lt)], axis=1)
    yg = _gelu(y)
    gate = jax.nn.sigmoid(jnp.dot(yg.astype(BF16), wglu_ref[...], preferred_element_type=F32))
    ys = (yg * gate).astype(BF16)
    mix = jnp.dot(ys, wout_ref[0:D_SSM, :], preferred_element_type=F32)
    mix = mix + jnp.dot(f_ref[...].reshape(bsz * ts, D_FNO), wout_ref[D_SSM:, :],
                        preferred_element_type=F32)
    x1 = x_ref[...].reshape(bsz * ts, D_MODEL) + mix
    x1_ref[...] = x1.reshape(bsz, ts, D_MODEL)
    h2t = _rms(x1, g2_ref[...]).T.astype(BF16)
    h2_ref[...] = h2t
    sc_ref[...] = jnp.dot(wsc_ref[...], h2t, preferred_element_type=F32)


def _merge(x, y3, f, wglu, wout, g2, wsc_t, ts):
    bsz, seq, d = x.shape
    n = bsz * seq
    nsc = wsc_t.shape[0]
    tm = bsz * ts
    ct = ts // CHUNK
    return pl.pallas_call(
        functools.partial(_merge_kernel, bsz=bsz, ts=ts),
        grid=(seq // ts,),
        in_specs=[
            pl.BlockSpec((bsz, ts, d), lambda i: (0, i, 0)),
            pl.BlockSpec((CHUNK, ct, bsz, D_SSM), lambda i: (0, i, 0, 0)),
            pl.BlockSpec((bsz, ts, D_FNO), lambda i: (0, i, 0)),
            pl.BlockSpec((D_SSM, D_SSM), lambda i: (0, 0)),
            pl.BlockSpec((d, d), lambda i: (0, 0)),
            pl.BlockSpec((1, d), lambda i: (0, 0)),
            pl.BlockSpec((nsc, d), lambda i: (0, 0)),
        ],
        out_specs=[
            pl.BlockSpec((bsz, ts, d), lambda i: (0, i, 0)),
            pl.BlockSpec((d, tm), lambda i: (0, i)),
            pl.BlockSpec((nsc, tm), lambda i: (0, i)),
        ],
        out_shape=[
            jax.ShapeDtypeStruct((bsz, seq, d), F32),
            jax.ShapeDtypeStruct((d, n), BF16),
            jax.ShapeDtypeStruct((nsc, n), F32),
        ],
        scratch_shapes=[pltpu.VMEM((D_SSM // LANES, tm, LANES), F32)],
        compiler_params=pltpu.CompilerParams(
            dimension_semantics=("parallel",), vmem_limit_bytes=VMEM_LIMIT),
        name="merge",
    )(x, y3, f, wglu, wout, g2, wsc_t)


_PARTIAL_ROWS = ((2, 5), (3, 4), (4, 3), (5, 2), (6, 2), (7, 2))


def _dup_bf16(v):
    hi = pltpu.bitcast(v.astype(BF16).astype(F32), jnp.uint32)
    return hi | (hi >> 16)


def _oddeven_merge_sort_pairs(n):
    def merge(lo, hi, r):
        step = r * 2
        if step < hi - lo:
            yield from merge(lo, hi, step)
            yield from merge(lo + r, hi, step)
            yield from ((i, i + r) for i in range(lo + r, hi - r, step))
        else:
            yield (lo, lo + r)

    def sort(lo, hi):
        if hi - lo >= 1:
            mid = lo + (hi - lo) // 2
            yield from sort(lo, mid)
            yield from sort(mid + 1, hi)
            yield from merge(lo, hi, 1)
    return tuple(sort(0, n - 1))


def _exchange(vs, i, j):
    vs[i], vs[j] = jnp.maximum(vs[i], vs[j]), jnp.minimum(vs[i], vs[j])


def _top16_sorted(s):
    n = PEER_TOPK
    assert s.shape[0] == n * SUBLANES
    vs = [s[SUBLANES * r:SUBLANES * (r + 1), :] for r in range(n)]
    for i, j in _oddeven_merge_sort_pairs(n):
        _exchange(vs, i, j)
    dist = SUBLANES // 2
    while dist:
        vs = [jnp.maximum(vs[k], pltpu.roll(vs[n - 1 - k], dist, 0)) for k in range(n)]
        stride = n // 2
        while stride:
            for k in range(n):
                if not k & stride:
                    _exchange(vs, k, k + stride)
            stride //= 2
        dist //= 2
    return vs


def _gate_kernel(sc_ref, al_ref, ll_ref, be_ref, r2_ref, v_scr):
    neg = -jnp.inf
    riota = lax.broadcasted_iota(jnp.int32, (SUBLANES, GATE_LANES), 0)

    def per_head(h, c):
        s1 = sc_ref[pl.ds(pl.multiple_of(2 * h * PEER_NKEYS, PEER_NKEYS), PEER_NKEYS), :]
        s2 = sc_ref[pl.ds(pl.multiple_of((2 * h + 1) * PEER_NKEYS, PEER_NKEYS), PEER_NKEYS), :]
        out_rows = pl.ds(pl.multiple_of(h * PEER_NKEYS, PEER_NKEYS), PEER_NKEYS)
        for p, s in enumerate((s1, s2)):
            for k, v in enumerate(_top16_sorted(s)):
                v_scr[p, k:k + 1, :] = v[0:1, :]
        rank2 = jnp.full((PEER_NKEYS, GATE_LANES), float(PEER_TOPK), F32)
        for k in reversed(range(PEER_TOPK)):
            rank2 = jnp.where(s2 >= v_scr[1, k:k + 1, :], float(k), rank2)
        v1a = v_scr[0, 0:8, :]
        v1b = v_scr[0, 8:16, :]
        v2a = v_scr[1, 0:8, :]
        v2b = v_scr[1, 8:16, :]
        top1 = v_scr[0, 0:1, :]
        top2 = v_scr[1, 0:1, :]
        blocks = [top1 + v2a, top1 + v2b, v_scr[0, 1:2, :] + v2a]
        for a, nb in _PARTIAL_ROWS:
            blocks.append(jnp.where(riota < nb, v_scr[0, a:a + 1, :] + v2a, neg))
        blocks.append(v1b + top2)
        pad = jnp.full((SUBLANES, GATE_LANES), neg, F32)
        blocks.extend([pad] * (PEER_TOPK - len(blocks)))
        best = _top16_sorted(jnp.concatenate(blocks, axis=0))
        mtop = top1 + top2
        z = jnp.exp(best[0] - mtop)
        for k in range(1, PEER_TOPK):
            z = z + jnp.exp(best[k] - mtop)
        z = z[0:1, :]
        tau = best[PEER_TOPK - 1][0:1, :]
        len_a = jnp.zeros((PEER_TOPK, GATE_LANES), F32)
        for b in range(PEER_TOPK):
            len_a = jnp.where((v_scr[0] + v_scr[1, b:b + 1, :]) >= tau, float(b + 1), len_a)
        v_scr[2] = len_a
        len1 = jnp.zeros((PEER_NKEYS, GATE_LANES), F32)
        for a in range(PEER_TOPK):
            len1 = jnp.where(s1 == v_scr[0, a:a + 1, :], v_scr[2, a:a + 1, :], len1)
        al_ref[out_rows, :] = _dup_bf16(jnp.exp(s1 - top1) * (1.0 / z))
        ll_ref[out_rows, :] = _dup_bf16(len1)
        be_ref[out_rows, :] = jnp.exp(s2 - top2).astype(BF16)
        r2_ref[out_rows, :] = rank2.astype(BF16)
        return c
    lax.fori_loop(0, PEER_HEADS, per_head, 0)


def _gate_prep(sc_t):
    nsc, n = sc_t.shape
    rows = PEER_HEADS * PEER_NKEYS
    spec = pl.BlockSpec((rows, GATE_LANES), lambda i: (0, i))
    return pl.pallas_call(
        _gate_kernel,
        grid=(n // GATE_LANES,),
        in_specs=[pl.BlockSpec((nsc, GATE_LANES), lambda i: (0, i))],
        out_specs=[spec, spec, spec, spec],
        out_shape=[
            jax.ShapeDtypeStruct((rows, n), jnp.uint32),
            jax.ShapeDtypeStruct((rows, n), jnp.uint32),
            jax.ShapeDtypeStruct((rows, n), BF16),
            jax.ShapeDtypeStruct((rows, n), BF16),
        ],
        scratch_shapes=[pltpu.VMEM((3, PEER_TOPK, GATE_LANES), F32)],
        compiler_params=pltpu.CompilerParams(
            dimension_semantics=("parallel",), vmem_limit_bytes=VMEM_LIMIT),
        name="gate_prep",
    )(sc_t)


def _peer_kernel(h2_ref, x1_ref, al_ref, ll_ref, be_ref, r2_ref, u_ref, vt_ref, gf_ref, o_ref,
                 pre_scr, a_scr, acc_scr):
    e = pl.program_id(1)
    n_e = pl.num_programs(1)
    rows_per_tile = TE_PEER // PEER_NKEYS
    n_lane_chunks = TT_PEER // PEER_LANE_CHUNK
    packed = (PEER_NKEYS // BF16_ROWS, BF16_ROWS, PEER_LANE_CHUNK)

    @pl.when(e == 0)
    def _init():
        acc_scr[...] = jnp.zeros_like(acc_scr)

    pre_scr[...] = jnp.dot(u_ref[...], h2_ref[...], preferred_element_type=F32)

    for lc in range(n_lane_chunks):
        ls = slice(lc * PEER_LANE_CHUNK, (lc + 1) * PEER_LANE_CHUNK)
        al_blk, ll_blk = [], []
        for h in range(PEER_HEADS):
            i_rows = pl.ds(pl.multiple_of(h * PEER_NKEYS + e * rows_per_tile, rows_per_tile),
                           rows_per_tile)
            al_blk.append(al_ref[i_rows, ls])
            ll_blk.append(ll_ref[i_rows, ls])
        for il in range(rows_per_tile):
            rs = slice(il * PEER_NKEYS, (il + 1) * PEER_NKEYS)
            g = jnp.zeros(packed, BF16)
            for h in range(PEER_HEADS):
                hs = slice(h * PEER_NKEYS, (h + 1) * PEER_NKEYS)
                len1 = pltpu.bitcast(jnp.broadcast_to(ll_blk[h][il:il + 1, :],
                                                      (SUBLANES, PEER_LANE_CHUNK)), BF16)
                alpha = pltpu.bitcast(jnp.broadcast_to(al_blk[h][il:il + 1, :],
                                                       (SUBLANES, PEER_LANE_CHUNK)), BF16)
                keep = r2_ref[hs, ls].reshape(packed) < len1[None]
                g = g + jnp.where(keep, be_ref[hs, ls].reshape(packed) * alpha[None], 0.0)
            act = _gelu(pre_scr[rs, ls]).astype(BF16).reshape(packed) * g
            a_scr[rs, ls] = act.reshape(PEER_NKEYS, PEER_LANE_CHUNK)

    acc_scr[...] += jnp.dot(vt_ref[...], a_scr[...], preferred_element_type=F32)

    @pl.when(e == n_e - 1)
    def _epilogue():
        x2 = x1_ref[...].reshape(TT_PEER, D_MODEL) + acc_scr[...].T
        o_ref[...] = _rms(x2, gf_ref[...]).reshape(o_ref.shape)


def _peer(h2t, x1, al, ll, be, r2, u, v_t, gf, ts):
    bsz, seq, d = x1.shape
    rows = PEER_HEADS * PEER_NKEYS
    tt, te = TT_PEER, TE_PEER
    assert tt == bsz * ts, "PEER token tile = one (all batches x ts positions) tile"
    gate_spec = pl.BlockSpec((rows, tt), lambda t, e: (0, t))
    return pl.pallas_call(
        _peer_kernel,
        grid=(seq // ts, PEER_EXPERTS // te),
        in_specs=[
            pl.BlockSpec((d, tt), lambda t, e: (0, t)),
            pl.BlockSpec((bsz, ts, d), lambda t, e: (0, t, 0)),
            gate_spec, gate_spec, gate_spec, gate_spec,
            pl.BlockSpec((te, d), lambda t, e: (e, 0)),
            pl.BlockSpec((d, te), lambda t, e: (0, e)),
            pl.BlockSpec((1, d), lambda t, e: (0, 0)),
        ],
        out_specs=pl.BlockSpec((bsz, ts, d), lambda t, e: (0, t, 0)),
        out_shape=jax.ShapeDtypeStruct((bsz, seq, d), F32),
        scratch_shapes=[
            pltpu.VMEM((te, tt), F32),
            pltpu.VMEM((te, tt), BF16),
            pltpu.VMEM((d, tt), F32),
        ],
        compiler_params=pltpu.CompilerParams(
            dimension_semantics=("parallel", "arbitrary"), vmem_limit_bytes=VMEM_LIMIT),
        name="peer",
    )(h2t, x1, al, ll, be, r2, u, v_t, gf)


def _cast_kernel(x_ref, o_ref, *, transpose):
    x = x_ref[...]
    o_ref[...] = (x.T if transpose else x).astype(BF16)


def _to_bf16(table, transpose):
    n_rows, d = table.shape
    blk = TE_PEER
    if transpose:
        out_shape, out_spec = (d, n_rows), pl.BlockSpec((d, blk), lambda i: (0, i))
    else:
        out_shape, out_spec = (n_rows, d), pl.BlockSpec((blk, d), lambda i: (i, 0))
    return pl.pallas_call(
        functools.partial(_cast_kernel, transpose=transpose),
        grid=(n_rows // blk,),
        in_specs=[pl.BlockSpec((blk, d), lambda i: (i, 0))],
        out_specs=out_spec,
        out_shape=jax.ShapeDtypeStruct(out_shape, BF16),
        compiler_params=pltpu.CompilerParams(
            dimension_semantics=("parallel",), vmem_limit_bytes=VMEM_LIMIT),
        name="table_cast",
    )(table)


def _fourier_weights(w_in_f, w_fourier, seq):
    kc = jnp.arange(FOURIER_GROUP, dtype=jnp.int32)
    ang_c = (2.0 * math.pi / FOURIER_GROUP) * ((kc[:, None] * kc[None, :]) % FOURIER_GROUP).astype(F32)
    scale = 1.0 / math.sqrt(seq * FOURIER_GROUP)
    wf = w_fourier.astype(F32)
    wc = jnp.einsum("ck,gkd->gcd", jnp.cos(ang_c), wf, precision=_HI) * scale
    ws = jnp.einsum("ck,gkd->gcd", jnp.sin(ang_c), wf, precision=_HI) * scale
    w_in_g = w_in_f.astype(F32).reshape(D_MODEL, N_FOURIER_GROUPS, FOURIER_GROUP)
    w_p = jnp.einsum("mgc,gcd->mgd", w_in_g, wc, precision=_HI).reshape(D_MODEL, D_FNO)
    w_q = jnp.einsum("mgc,gcd->mgd", w_in_g, ws, precision=_HI).reshape(D_MODEL, D_FNO)
    return w_p, w_q


def kernel(x, norm1_g, w_in, ssm_a_re, ssm_a_im, ssm_log_step, ssm_b_re, ssm_b_im, ssm_c_re,
           ssm_c_im, ssm_d, w_glu, w_fourier, w_out, norm2_g, w_query, sub_keys, expert_u,
           expert_v, final_g):
    bsz, seq, d = x.shape
    n = bsz * seq
    n_chunks = seq // CHUNK
    ts = TM_IN // bsz
    assert d == D_MODEL and seq % (2 * TK_DFT) == 0 and TM_IN % bsz == 0 and n % TT_PEER == 0
    assert ts % CHUNK == 0 and ts % BF16_ROWS == 0 and seq % ts == 0
    assert norm1_g.shape[0] == 1, "single-layer block"
    layer = 0

    w_p, w_q = _fourier_weights(w_in[layer][:, D_SSM:], w_fourier[layer], seq)
    w_all = jnp.concatenate([w_in[layer][:, :D_SSM].astype(F32), w_p, w_q], axis=1).astype(BF16)
    wb, t_mat, wca, wcb, lam16 = _s5_weights(
        ssm_a_re[layer], ssm_a_im[layer], ssm_log_step[layer], ssm_b_re[layer], ssm_b_im[layer],
        ssm_c_re[layer], ssm_c_im[layer], ssm_d[layer])
    wq4 = w_query[layer].astype(F32).reshape(D_MODEL, PEER_HEADS, 2, PEER_HALF)
    wsc_t = jnp.einsum("dhpe,hpke->hpkd", wq4, sub_keys[layer].astype(F32), precision=_HI)
    wsc_t = wsc_t.reshape(PEER_HEADS * 2 * PEER_NKEYS, D_MODEL).astype(BF16)
    u_bf = _to_bf16(expert_u[layer], transpose=False)
    v_t = _to_bf16(expert_v[layer], transpose=True)

    u3, p, q = _inproj(x, norm1_g[layer].reshape(1, d).astype(F32), w_all, ts)

    yg = _s5(_to_groups(u3), wb, t_mat, wca, wcb, lam16, n_chunks, bsz)
    y3 = _from_groups(yg, n_chunks, bsz)

    f = jnp.concatenate(_seq_dft(p, q, bsz), axis=1)

    x1, h2t, sc_t = _merge(x, y3, f, w_glu[layer].astype(BF16), w_out[layer].astype(BF16),
                           norm2_g[layer].reshape(1, d).astype(F32), wsc_t, ts)

    al, ll, be, r2 = _gate_prep(sc_t)
    return _peer(h2t, x1, al, ll, be, r2, u_bf, v_t, final_g.reshape(1, d).astype(F32), ts)
```

```python
import functools
import math

import jax
import jax.numpy as jnp
from jax import lax
from jax.experimental import pallas as pl
from jax.experimental.pallas import tpu as pltpu

F32 = jnp.float32
BF16 = jnp.bfloat16

D_MODEL = 1024
D_SSM = 512
D_FNO = 512
SSM_GROUP = 16
N_SSM_GROUPS = 32
SSM_STATE = 64
N_FOURIER_GROUPS = 4
FOURIER_GROUP = 128
PEER_HEADS = 8
PEER_NKEYS = 128
PEER_EXPERTS = PEER_NKEYS * PEER_NKEYS
PEER_TOPK = 16
PEER_HALF = 128
RMS_EPS = 1e-6

CHUNK = 16
CHUNK_W = CHUNK * SSM_GROUP
LANES = 128
VMEM_LIMIT = 56 * 1024 * 1024

TM_IN = 512
TK_DFT = 512
TT_PEER = 512
SUBLANES = 8
BF16_ROWS = 16
PEER_LANE_CHUNK = 256
GATE_LANES = 256
TE_PEER = 2 * SUBLANES * PEER_NKEYS

_HI = lax.Precision.HIGHEST


def _rms(x, g):
    return x * lax.rsqrt(jnp.mean(x * x, axis=-1, keepdims=True) + RMS_EPS) * g


def _gelu(x):
    return 0.5 * x * (1.0 + lax.erf(x * math.sqrt(0.5)))


def _inproj_kernel(x_ref, g_ref, w_ref, u3_ref, p_ref, q_ref, zu_scr, *, bsz, ts):
    h = _rms(x_ref[...].reshape(bsz * ts, D_MODEL), g_ref[...])
    z = jnp.dot(h.astype(BF16), w_ref[...], preferred_element_type=F32)
    for b in range(bsz):
        rows = slice(b * ts, (b + 1) * ts)
        p_ref[:, b * D_FNO:(b + 1) * D_FNO] = z[rows, D_SSM:D_SSM + D_FNO].astype(BF16)
        q_ref[:, b * D_FNO:(b + 1) * D_FNO] = z[rows, D_SSM + D_FNO:].astype(BF16)
    n_lt = D_SSM // LANES
    for j in range(n_lt):
        zu_scr[j] = z[:, j * LANES:(j + 1) * LANES]
    for t in range(CHUNK):
        for cl in range(ts // CHUNK):
            rows = [zu_scr[j, pl.ds(cl * CHUNK + t, bsz, stride=ts), :] for j in range(n_lt)]
            u3_ref[t, cl] = jnp.concatenate(rows, axis=1).astype(BF16)


def _inproj(x, g, w_all, ts):
    bsz, seq, d = x.shape
    nw = w_all.shape[1]
    ct = ts // CHUNK
    return pl.pallas_call(
        functools.partial(_inproj_kernel, bsz=bsz, ts=ts),
        grid=(seq // ts,),
        in_specs=[
            pl.BlockSpec((bsz, ts, d), lambda i: (0, i, 0)),
            pl.BlockSpec((1, d), lambda i: (0, 0)),
            pl.BlockSpec((d, nw), lambda i: (0, 0)),
        ],
        out_specs=[
            pl.BlockSpec((CHUNK, ct, bsz, D_SSM), lambda i: (0, i, 0, 0)),
            pl.BlockSpec((ts, bsz * D_FNO), lambda i: (i, 0)),
            pl.BlockSpec((ts, bsz * D_FNO), lambda i: (i, 0)),
        ],
        out_shape=[
            jax.ShapeDtypeStruct((CHUNK, seq // CHUNK, bsz, D_SSM), BF16),
            jax.ShapeDtypeStruct((seq, bsz * D_FNO), BF16),
            jax.ShapeDtypeStruct((seq, bsz * D_FNO), BF16),
        ],
        scratch_shapes=[pltpu.VMEM((D_SSM // LANES, bsz * ts, LANES), F32)],
        compiler_params=pltpu.CompilerParams(
            dimension_semantics=("parallel",), vmem_limit_bytes=VMEM_LIMIT),
        name="inproj",
    )(x, g, w_all)


def _swap_lane_groups(arrs):
    n = LANES // SSM_GROUP
    grp = lax.broadcasted_iota(jnp.int32, arrs[0].shape, 1) // SSM_GROUP
    arrs = list(arrs)
    s = n // 2
    while s:
        low = (grp & s) == 0
        nxt = list(arrs)
        for p0 in range(n):
            if p0 & s:
                continue
            p1 = p0 | s
            nxt[p0] = jnp.where(low, arrs[p0], pltpu.roll(arrs[p1], SSM_GROUP * s, 1))
            nxt[p1] = jnp.where(low, pltpu.roll(arrs[p0], LANES - SSM_GROUP * s, 1), arrs[p1])
        arrs = nxt
        s //= 2
    return arrs


def _to_groups_kernel(u3_ref, xg_ref, *, rows):
    n = LANES // SSM_GROUP
    halves = []
    for th in range(CHUNK // n):
        a = [pltpu.bitcast(u3_ref[th * n + q].reshape(rows, LANES), jnp.uint32) for q in range(n)]
        halves.append(_swap_lane_groups(a))
    for g in range(n):
        xg_ref[g] = pltpu.bitcast(jnp.concatenate([hv[g] for hv in halves], axis=1), BF16)


def _from_groups_kernel(yg_ref, y3_ref, *, cb, bsz):
    n = LANES // SSM_GROUP
    words = [pltpu.bitcast(yg_ref[g], jnp.uint32) for g in range(n)]
    for th in range(CHUNK // n):
        b = _swap_lane_groups([w[:, th * LANES:(th + 1) * LANES] for w in words])
        for tl in range(n):
            y3_ref[th * n + tl] = pltpu.bitcast(b[tl], BF16).reshape(cb, bsz, LANES)


def _regroup_blocks(n_chunks, bsz):
    cb = min(n_chunks, TM_IN // bsz)
    return cb, cb * bsz


def _to_groups(u3):
    _, n_chunks, bsz, _ = u3.shape
    cb, rows = _regroup_blocks(n_chunks, bsz)
    n = LANES // SSM_GROUP
    return pl.pallas_call(
        functools.partial(_to_groups_kernel, rows=rows),
        grid=(n_chunks // cb, D_SSM // LANES),
        in_specs=[pl.BlockSpec((CHUNK, cb, bsz, LANES), lambda c, g: (0, c, 0, g))],
        out_specs=pl.BlockSpec((n, rows, CHUNK_W), lambda c, g: (g, c, 0)),
        out_shape=jax.ShapeDtypeStruct((N_SSM_GROUPS, n_chunks * bsz, CHUNK_W), BF16),
        compiler_params=pltpu.CompilerParams(
            dimension_semantics=("parallel", "parallel"), vmem_limit_bytes=VMEM_LIMIT),
        name="s5_to_groups",
    )(u3)


def _from_groups(yg, n_chunks, bsz):
    cb, rows = _regroup_blocks(n_chunks, bsz)
    n = LANES // SSM_GROUP
    return pl.pallas_call(
        functools.partial(_from_groups_kernel, cb=cb, bsz=bsz),
        grid=(n_chunks // cb, D_SSM // LANES),
        in_specs=[pl.BlockSpec((n, rows, CHUNK_W), lambda c, g: (g, c, 0))],
        out_specs=pl.BlockSpec((CHUNK, cb, bsz, LANES), lambda c, g: (0, c, 0, g)),
        out_shape=jax.ShapeDtypeStruct((CHUNK, n_chunks, bsz, D_SSM), BF16),
        compiler_params=pltpu.CompilerParams(
            dimension_semantics=("parallel", "parallel"), vmem_limit_bytes=VMEM_LIMIT),
        name="s5_from_groups",
    )(yg)


def _s5_kernel(x_ref, wb_ref, t_ref, wca_ref, wcb_ref, lam_ref, y_ref, s_scr, ha_scr, hb_scr,
               *, n_chunks, bsz):
    rows = n_chunks * bsz
    rblk = min(rows, 512)
    def summarize(r, c):
        rs = pl.ds(pl.multiple_of(r * rblk, rblk), rblk)
        s_scr[rs, :] = jnp.dot(x_ref[rs, :], wb_ref[...], preferred_element_type=F32)
        return c
    lax.fori_loop(0, rows // rblk, summarize, 0)

    lr = lam_ref[0:1, :]
    li = lam_ref[1:2, :]
    is_fwd = lax.broadcasted_iota(jnp.int32, (bsz, LANES), 1) < SSM_STATE

    def step(k, carry):
        hre, him = carry
        rf = pl.ds(pl.multiple_of(k * bsz, bsz), bsz)
        rb = pl.ds(pl.multiple_of((n_chunks - 1 - k) * bsz, bsz), bsz)
        ha_scr[rf, 0:LANES] = hre
        ha_scr[rf, LANES:2 * LANES] = him
        hb_scr[rb, 0:LANES] = hre
        hb_scr[rb, LANES:2 * LANES] = him
        sre = jnp.where(is_fwd, s_scr[rf, 0:LANES], s_scr[rb, 0:LANES])
        sim = jnp.where(is_fwd, s_scr[rf, LANES:2 * LANES], s_scr[rb, LANES:2 * LANES])
        nre = lr * hre - li * him + sre
        nim = lr * him + li * hre + sim
        return nre, nim
    zero = jnp.zeros((bsz, LANES), F32)
    lax.fori_loop(0, n_chunks, step, (zero, zero), unroll=4)

    def emit(r, c):
        rs = pl.ds(pl.multiple_of(r * rblk, rblk), rblk)
        y = jnp.dot(x_ref[rs, :], t_ref[...], preferred_element_type=F32)
        y = y + jnp.dot(ha_scr[rs, :].astype(BF16), wca_ref[...], preferred_element_type=F32)
        y = y + jnp.dot(hb_scr[rs, :].astype(BF16), wcb_ref[...], preferred_element_type=F32)
        y_ref[rs, :] = y.astype(BF16)
        return c
    lax.fori_loop(0, rows // rblk, emit, 0)


def _s5(xg, wb, tt, wca, wcb, lam16, n_chunks, bsz):
    ng, rows, w = xg.shape
    wspec = pl.BlockSpec((None, w, w), lambda g: (g, 0, 0))
    return pl.pallas_call(
        functools.partial(_s5_kernel, n_chunks=n_chunks, bsz=bsz),
        grid=(ng,),
        in_specs=[
            pl.BlockSpec((None, rows, w), lambda g: (g, 0, 0)),
            wspec, wspec, wspec, wspec,
            pl.BlockSpec((None, 2, LANES), lambda g: (g, 0, 0)),
        ],
        out_specs=pl.BlockSpec((None, rows, w), lambda g: (g, 0, 0)),
        out_shape=jax.ShapeDtypeStruct((ng, rows, w), BF16),
        scratch_shapes=[
            pltpu.VMEM((rows, w), F32),
            pltpu.VMEM((rows, w), F32),
            pltpu.VMEM((rows, w), F32),
        ],
        compiler_params=pltpu.CompilerParams(
            dimension_semantics=("parallel",), vmem_limit_bytes=VMEM_LIMIT),
        name="s5",
    )(xg, wb, tt, wca, wcb, lam16)


def _s5_weights(a_re, a_im, log_step, b_re, b_im, c_re, c_im, d_skip):
    g_n, p_n, cg = N_SSM_GROUPS, SSM_STATE, SSM_GROUP
    tau = jnp.arange(CHUNK + 1, dtype=F32)

    def direction(d):
        lam = lax.complex(a_re[d].astype(F32), a_im[d].astype(F32))
        step = jnp.exp(log_step[d].astype(F32))[:, None]
        lam_step = lam * step
        lam_bar = jnp.exp(lam_step)
        b_bar = ((lam_bar - 1.0) / lam)[..., None] * lax.complex(b_re[d].astype(F32), b_im[d].astype(F32))
        c = lax.complex(c_re[d].astype(F32), c_im[d].astype(F32))
        lam_pow = jnp.exp(lam_step[None] * tau[:, None, None])
        kern = jnp.einsum("gop,tgp,gpi->tgoi", c, lam_pow[:CHUNK], b_bar, precision=_HI).real
        return lam_pow, b_bar, c, kern

    lp_f, bb_f, c_f, k_f = direction(0)
    lp_b, bb_b, c_b, k_b = direction(1)

    t_idx = jnp.arange(CHUNK)
    lag = t_idx[None, :] - t_idx[:, None]
    kf = jnp.where((lag >= 0)[:, :, None, None, None], k_f[jnp.clip(lag, 0, CHUNK - 1)], 0.0)
    kb = jnp.where((lag <= 0)[:, :, None, None, None], k_b[jnp.clip(-lag, 0, CHUNK - 1)], 0.0)
    eye_t = jnp.eye(CHUNK, dtype=F32)[:, :, None, None, None]
    eye_c = jnp.eye(cg, dtype=F32)[None, None, None]
    dmat = eye_t * eye_c * d_skip.astype(F32)[None, None, :, :, None]
    tot = kf + kb + dmat
    t_mat = tot.transpose(2, 0, 4, 1, 3).reshape(g_n, CHUNK_W, CHUNK_W)

    wb_f = jnp.einsum("tgp,gpi->gtip", lp_f[:CHUNK][::-1], bb_f).reshape(g_n, CHUNK_W, p_n)
    wb_b = jnp.einsum("tgp,gpi->gtip", lp_b[:CHUNK], bb_b).reshape(g_n, CHUNK_W, p_n)
    wb = jnp.concatenate([wb_f.real, wb_b.real, wb_f.imag, wb_b.imag], axis=-1)

    wc_f = jnp.einsum("gop,tgp->gpto", c_f, lp_f[1:CHUNK + 1]).reshape(g_n, p_n, CHUNK_W)
    wc_b = jnp.einsum("gop,tgp->gpto", c_b, lp_b[1:CHUNK + 1][::-1]).reshape(g_n, p_n, CHUNK_W)
    zeros = jnp.zeros_like(wc_f.real)
    wca = jnp.concatenate([wc_f.real, zeros, -wc_f.imag, zeros], axis=1)
    wcb = jnp.concatenate([zeros, wc_b.real, zeros, -wc_b.imag], axis=1)

    lam16 = jnp.stack([
        jnp.concatenate([lp_f[CHUNK].real, lp_b[CHUNK].real], axis=-1),
        jnp.concatenate([lp_f[CHUNK].imag, lp_b[CHUNK].imag], axis=-1)], axis=1)
    return wb.astype(BF16), t_mat.astype(BF16), wca.astype(BF16), wcb.astype(BF16), lam16


def _dft_kernel(p_ref, q_ref, flip_ref, lo_ref, hi_ref, cd_scr, sd_scr, ct_scr, st_scr, *, seq):
    i = pl.program_id(0)
    b = pl.program_id(1)
    rblk = BF16_ROWS
    n_rows = TK_DFT + BF16_ROWS
    n_idx = lax.broadcasted_iota(jnp.int32, (rblk, seq), 1)
    w0 = 2.0 * math.pi / seq

    @pl.when((i == 0) & (b == 0))
    def _base_tables():
        def rows(r, c):
            rs = pl.ds(pl.multiple_of(r * rblk, rblk), rblk)
            dk = lax.broadcasted_iota(jnp.int32, (rblk, seq), 0) + r * rblk
            ang = w0 * ((n_idx * dk) & (seq - 1)).astype(F32)
            cd_scr[rs, :] = jnp.cos(ang)
            sd_scr[rs, :] = jnp.sin(ang)
            return c
        lax.fori_loop(0, n_rows // rblk, rows, 0)

    @pl.when(b == 0)
    def _tile_tables():
        ang0 = w0 * ((n_idx[0:1, :] * (i * TK_DFT)) & (seq - 1)).astype(F32)
        cb = jnp.cos(ang0)
        sb = jnp.sin(ang0)

        def rows(r, c):
            rs = pl.ds(pl.multiple_of(r * rblk, rblk), rblk)
            cd = cd_scr[rs, :]
            sd = sd_scr[rs, :]
            ct_scr[rs, :] = (cb * cd - sb * sd).astype(BF16)
            st_scr[rs, :] = (sb * cd + cb * sd).astype(BF16)
            return c
        lax.fori_loop(0, n_rows // rblk, rows, 0)

    ck = jnp.dot(ct_scr[...], p_ref[...], preferred_element_type=F32)
    sk = jnp.dot(st_scr[...], q_ref[...], preferred_element_type=F32)
    lo_ref[...] = (ck - sk)[0:TK_DFT, :].astype(BF16)
    hi_ref[...] = jnp.dot(flip_ref[...], (ck + sk).astype(BF16),
                          preferred_element_type=F32).astype(BF16)


def _seq_dft(p, q, bsz):
    seq = p.shape[0]
    assert seq & (seq - 1) == 0, "angle reduction uses a power-of-two sequence length"
    n_half = seq // (2 * TK_DFT)
    n_rows = TK_DFT + BF16_ROWS
    flip = (jnp.arange(TK_DFT)[:, None] + jnp.arange(n_rows)[None, :] == TK_DFT).astype(BF16)
    half = jax.ShapeDtypeStruct((bsz, seq // 2, D_FNO), BF16)
    return pl.pallas_call(
        functools.partial(_dft_kernel, seq=seq),
        grid=(n_half, bsz),
        in_specs=[
            pl.BlockSpec((seq, D_FNO), lambda i, b: (0, b)),
            pl.BlockSpec((seq, D_FNO), lambda i, b: (0, b)),
            pl.BlockSpec((TK_DFT, n_rows), lambda i, b: (0, 0)),
        ],
        out_specs=[
            pl.BlockSpec((None, TK_DFT, D_FNO), lambda i, b: (b, i, 0)),
            pl.BlockSpec((None, TK_DFT, D_FNO), lambda i, b: (b, n_half - 1 - i, 0)),
        ],
        out_shape=[half, half],
        scratch_shapes=[
            pltpu.VMEM((n_rows, seq), F32),
            pltpu.VMEM((n_rows, seq), F32),
            pltpu.VMEM((n_rows, seq), BF16),
            pltpu.VMEM((n_rows, seq), BF16),
        ],
        compiler_params=pltpu.CompilerParams(
            dimension_semantics=("arbitrary", "arbitrary"), vmem_limit_bytes=VMEM_LIMIT),
        name="seq_dft",
    )(p, q, flip)


def _merge_kernel(x_ref, y3_ref, f_ref, wglu_ref, wout_ref, g2_ref, wsc_ref,
                  x1_ref, h2_ref, sc_ref, y_scr, *, bsz, ts):
    n_lt = D_SSM // LANES
    for t in range(CHUNK):
        for cl in range(ts // CHUNK):
            yv = y3_ref[t, cl].astype(F32)
            for j in range(n_lt):
                y_scr[j, pl.ds(cl * CHUNK + t, bsz, stride=ts), :] = yv[:, j * LANES:(j + 1) * LANES]
    y = jnp.concatenate([y_scr[j] for j in range(n_lt)], axis=1)
    yg = _gelu(y)
    gate = jax.nn.sigmoid(jnp.dot(yg.astype(BF16), wglu_ref[...], preferred_element_type=F32))
    ys = (yg * gate).astype(BF16)
    mix = jnp.dot(ys, wout_ref[0:D_SSM, :], preferred_element_type=F32)
    mix = mix + jnp.dot(f_ref[...].reshape(bsz * ts, D_FNO), wout_ref[D_SSM:, :],
                        preferred_element_type=F32)
    x1 = x_ref[...].reshape(bsz * ts, D_MODEL) + mix
    x1_ref[...] = x1.reshape(bsz, ts, D_MODEL)
    h2t = _rms(x1, g2_ref[...]).T.astype(BF16)
    h2_ref[...] = h2t
    sc_ref[...] = jnp.dot(wsc_ref[...], h2t, preferred_element_type=F32)


def _merge(x, y3, f, wglu, wout, g2, wsc_t, ts):
    bsz, seq, d = x.shape
    n = bsz * seq
    nsc = wsc_t.shape[0]
    tm = bsz * ts
    ct = ts // CHUNK
    return pl.pallas_call(
        functools.partial(_merge_kernel, bsz=bsz, ts=ts),
        grid=(seq // ts,),
        in_specs=[
            pl.BlockSpec((bsz, ts, d), lambda i: (0, i, 0)),
            pl.BlockSpec((CHUNK, ct, bsz, D_SSM), lambda i: (0, i, 0, 0)),
            pl.BlockSpec((bsz, ts, D_FNO), lambda i: (0, i, 0)),
            pl.BlockSpec((D_SSM, D_SSM), lambda i: (0, 0)),
            pl.BlockSpec((d, d), lambda i: (0, 0)),
            pl.BlockSpec((1, d), lambda i: (0, 0)),
            pl.BlockSpec((nsc, d), lambda i: (0, 0)),
        ],
        out_specs=[
            pl.BlockSpec((bsz, ts, d), lambda i: (0, i, 0)),
            pl.BlockSpec((d, tm), lambda i: (0, i)),
            pl.BlockSpec((nsc, tm), lambda i: (0, i)),
        ],
        out_shape=[
            jax.ShapeDtypeStruct((bsz, seq, d), F32),
            jax.ShapeDtypeStruct((d, n), BF16),
            jax.ShapeDtypeStruct((nsc, n), F32),
        ],
        scratch_shapes=[pltpu.VMEM((D_SSM // LANES, tm, LANES), F32)],
        compiler_params=pltpu.CompilerParams(
            dimension_semantics=("parallel",), vmem_limit_bytes=VMEM_LIMIT),
        name="merge",
    )(x, y3, f, wglu, wout, g2, wsc_t)


_PARTIAL_ROWS = ((2, 5), (3, 4), (4, 3), (5, 2), (6, 2), (7, 2))


def _dup_bf16(v):
    hi = pltpu.bitcast(v.astype(BF16).astype(F32), jnp.uint32)
    return hi | (hi >> 16)


def _oddeven_merge_sort_pairs(n):
    def merge(lo, hi, r):
        step = r * 2
        if step < hi - lo:
            yield from merge(lo, hi, step)
            yield from merge(lo + r, hi, step)
            yield from ((i, i + r) for i in range(lo + r, hi - r, step))
        else:
            yield (lo, lo + r)

    def sort(lo, hi):
        if hi - lo >= 1:
            mid = lo + (hi - lo) // 2
            yield from sort(lo, mid)
            yield from sort(mid + 1, hi)
            yield from merge(lo, hi, 1)
    return tuple(sort(0, n - 1))


def _exchange(vs, i, j):
    vs[i], vs[j] = jnp.maximum(vs[i], vs[j]), jnp.minimum(vs[i], vs[j])


def _top16_sorted(s):
    n = PEER_TOPK
    assert s.shape[0] == n * SUBLANES
    vs = [s[SUBLANES * r:SUBLANES * (r + 1), :] for r in range(n)]
    for i, j in _oddeven_merge_sort_pairs(n):
        _exchange(vs, i, j)
    dist = SUBLANES // 2
    while dist:
        vs = [jnp.maximum(vs[k], pltpu.roll(vs[n - 1 - k], dist, 0)) for k in range(n)]
        stride = n // 2
        while stride:
            for k in range(n):
                if not k & stride:
                    _exchange(vs, k, k + stride)
            stride //= 2
        dist //= 2
    return vs


def _gate_kernel(sc_ref, al_ref, ll_ref, be_ref, r2_ref, v_scr):
    neg = -jnp.inf
    riota = lax.broadcasted_iota(jnp.int32, (SUBLANES, GATE_LANES), 0)

    def per_head(h, c):
        s1 = sc_ref[pl.ds(pl.multiple_of(2 * h * PEER_NKEYS, PEER_NKEYS), PEER_NKEYS), :]
        s2 = sc_ref[pl.ds(pl.multiple_of((2 * h + 1) * PEER_NKEYS, PEER_NKEYS), PEER_NKEYS), :]
        out_rows = pl.ds(pl.multiple_of(h * PEER_NKEYS, PEER_NKEYS), PEER_NKEYS)
        for p, s in enumerate((s1, s2)):
            for k, v in enumerate(_top16_sorted(s)):
                v_scr[p, k:k + 1, :] = v[0:1, :]
        rank2 = jnp.full((PEER_NKEYS, GATE_LANES), float(PEER_TOPK), F32)
        for k in reversed(range(PEER_TOPK)):
            rank2 = jnp.where(s2 >= v_scr[1, k:k + 1, :], float(k), rank2)
        v1a = v_scr[0, 0:8, :]
        v1b = v_scr[0, 8:16, :]
        v2a = v_scr[1, 0:8, :]
        v2b = v_scr[1, 8:16, :]
        top1 = v_scr[0, 0:1, :]
        top2 = v_scr[1, 0:1, :]
        blocks = [top1 + v2a, top1 + v2b, v_scr[0, 1:2, :] + v2a]
        for a, nb in _PARTIAL_ROWS:
            blocks.append(jnp.where(riota < nb, v_scr[0, a:a + 1, :] + v2a, neg))
        blocks.append(v1b + top2)
        pad = jnp.full((SUBLANES, GATE_LANES), neg, F32)
        blocks.extend([pad] * (PEER_TOPK - len(blocks)))
        best = _top16_sorted(jnp.concatenate(blocks, axis=0))
        mtop = top1 + top2
        z = jnp.exp(best[0] - mtop)
        for k in range(1, PEER_TOPK):
            z = z + jnp.exp(best[k] - mtop)
        z = z[0:1, :]
        tau = best[PEER_TOPK - 1][0:1, :]
        len_a = jnp.zeros((PEER_TOPK, GATE_LANES), F32)
        for b in range(PEER_TOPK):
            len_a = jnp.where((v_scr[0] + v_scr[1, b:b + 1, :]) >= tau, float(b + 1), len_a)
        v_scr[2] = len_a
        len1 = jnp.zeros((PEER_NKEYS, GATE_LANES), F32)
        for a in range(PEER_TOPK):
            len1 = jnp.where(s1 == v_scr[0, a:a + 1, :], v_scr[2, a:a + 1, :], len1)
        al_ref[out_rows, :] = _dup_bf16(jnp.exp(s1 - top1) * (1.0 / z))
        ll_ref[out_rows, :] = _dup_bf16(len1)
        be_ref[out_rows, :] = jnp.exp(s2 - top2).astype(BF16)
        r2_ref[out_rows, :] = rank2.astype(BF16)
        return c
    lax.fori_loop(0, PEER_HEADS, per_head, 0)


def _gate_prep(sc_t):
    nsc, n = sc_t.shape
    rows = PEER_HEADS * PEER_NKEYS
    spec = pl.BlockSpec((rows, GATE_LANES), lambda i: (0, i))
    return pl.pallas_call(
        _gate_kernel,
        grid=(n // GATE_LANES,),
        in_specs=[pl.BlockSpec((nsc, GATE_LANES), lambda i: (0, i))],
        out_specs=[spec, spec, spec, spec],
        out_shape=[
            jax.ShapeDtypeStruct((rows, n), jnp.uint32),
            jax.ShapeDtypeStruct((rows, n), jnp.uint32),
            jax.ShapeDtypeStruct((rows, n), BF16),
            jax.ShapeDtypeStruct((rows, n), BF16),
        ],
        scratch_shapes=[pltpu.VMEM((3, PEER_TOPK, GATE_LANES), F32)],
        compiler_params=pltpu.CompilerParams(
            dimension_semantics=("parallel",), vmem_limit_bytes=VMEM_LIMIT),
        name="gate_prep",
    )(sc_t)


def _peer_kernel(h2_ref, x1_ref, al_ref, ll_ref, be_ref, r2_ref, u_ref, vt_ref, gf_ref, o_ref,
                 pre_scr, a_scr, acc_scr):
    e = pl.program_id(1)
    n_e = pl.num_programs(1)
    rows_per_tile = TE_PEER // PEER_NKEYS
    n_lane_chunks = TT_PEER // PEER_LANE_CHUNK
    packed = (PEER_NKEYS // BF16_ROWS, BF16_ROWS, PEER_LANE_CHUNK)

    @pl.when(e == 0)
    def _init():
        acc_scr[...] = jnp.zeros_like(acc_scr)

    pre_scr[...] = jnp.dot(u_ref[...], h2_ref[...], preferred_element_type=F32)

    for lc in range(n_lane_chunks):
        ls = slice(lc * PEER_LANE_CHUNK, (lc + 1) * PEER_LANE_CHUNK)
        al_blk, ll_blk = [], []
        for h in range(PEER_HEADS):
            i_rows = pl.ds(pl.multiple_of(h * PEER_NKEYS + e * rows_per_tile, rows_per_tile),
                           rows_per_tile)
            al_blk.append(al_ref[i_rows, ls])
            ll_blk.append(ll_ref[i_rows, ls])
        for il in range(rows_per_tile):
            rs = slice(il * PEER_NKEYS, (il + 1) * PEER_NKEYS)
            g = None
            for h in range(PEER_HEADS):
                hs = slice(h * PEER_NKEYS, (h + 1) * PEER_NKEYS)
                len1 = pltpu.bitcast(jnp.broadcast_to(ll_blk[h][il:il + 1, :],
                                                      (SUBLANES, PEER_LANE_CHUNK)), BF16)
                alpha = pltpu.bitcast(jnp.broadcast_to(al_blk[h][il:il + 1, :],
                                                       (SUBLANES, PEER_LANE_CHUNK)), BF16)
                keep = r2_ref[hs, ls].reshape(packed) < len1[None]
                term = jnp.where(keep, be_ref[hs, ls].reshape(packed) * alpha[None], 0.0)
                g = term if g is None else g + term
            act = _gelu(pre_scr[rs, ls].astype(BF16)).reshape(packed) * g
            a_scr[rs, ls] = act.reshape(PEER_NKEYS, PEER_LANE_CHUNK)

    acc_scr[...] += jnp.dot(vt_ref[...], a_scr[...], preferred_element_type=F32)

    @pl.when(e == n_e - 1)
    def _epilogue():
        x2 = x1_ref[...].reshape(TT_PEER, D_MODEL) + acc_scr[...].T
        o_ref[...] = _rms(x2, gf_ref[...]).reshape(o_ref.shape)


def _peer(h2t, x1, al, ll, be, r2, u, v_t, gf, ts):
    bsz, seq, d = x1.shape
    rows = PEER_HEADS * PEER_NKEYS
    tt, te = TT_PEER, TE_PEER
    assert tt == bsz * ts, "PEER token tile = one (all batches x ts positions) tile"
    gate_spec = pl.BlockSpec((rows, tt), lambda t, e: (0, t))
    return pl.pallas_call(
        _peer_kernel,
        grid=(seq // ts, PEER_EXPERTS // te),
        in_specs=[
            pl.BlockSpec((d, tt), lambda t, e: (0, t)),
            pl.BlockSpec((bsz, ts, d), lambda t, e: (0, t, 0)),
            gate_spec, gate_spec, gate_spec, gate_spec,
            pl.BlockSpec((te, d), lambda t, e: (e, 0)),
            pl.BlockSpec((d, te), lambda t, e: (0, e)),
            pl.BlockSpec((1, d), lambda t, e: (0, 0)),
        ],
        out_specs=pl.BlockSpec((bsz, ts, d), lambda t, e: (0, t, 0)),
        out_shape=jax.ShapeDtypeStruct((bsz, seq, d), F32),
        scratch_shapes=[
            pltpu.VMEM((te, tt), F32),
            pltpu.VMEM((te, tt), BF16),
            pltpu.VMEM((d, tt), F32),
        ],
        compiler_params=pltpu.CompilerParams(
            dimension_semantics=("parallel", "arbitrary"), vmem_limit_bytes=VMEM_LIMIT),
        name="peer",
    )(h2t, x1, al, ll, be, r2, u, v_t, gf)


def _cast_kernel(x_ref, o_ref, *, transpose):
    x = x_ref[...]
    o_ref[...] = (x.T if transpose else x).astype(BF16)


def _to_bf16(table, transpose):
    n_rows, d = table.shape
    blk = TE_PEER
    if transpose:
        out_shape, out_spec = (d, n_rows), pl.BlockSpec((d, blk), lambda i: (0, i))
    else:
        out_shape, out_spec = (n_rows, d), pl.BlockSpec((blk, d), lambda i: (i, 0))
    return pl.pallas_call(
        functools.partial(_cast_kernel, transpose=transpose),
        grid=(n_rows // blk,),
        in_specs=[pl.BlockSpec((blk, d), lambda i: (i, 0))],
        out_specs=out_spec,
        out_shape=jax.ShapeDtypeStruct(out_shape, BF16),
        compiler_params=pltpu.CompilerParams(
            dimension_semantics=("parallel",), vmem_limit_bytes=VMEM_LIMIT),
        name="table_cast",
    )(table)


def _fourier_weights(w_in_f, w_fourier, seq):
    kc = jnp.arange(FOURIER_GROUP, dtype=jnp.int32)
    ang_c = (2.0 * math.pi / FOURIER_GROUP) * ((kc[:, None] * kc[None, :]) % FOURIER_GROUP).astype(F32)
    scale = 1.0 / math.sqrt(seq * FOURIER_GROUP)
    wf = w_fourier.astype(F32)
    wc = jnp.einsum("ck,gkd->gcd", jnp.cos(ang_c), wf, precision=_HI) * scale
    ws = jnp.einsum("ck,gkd->gcd", jnp.sin(ang_c), wf, precision=_HI) * scale
    w_in_g = w_in_f.astype(F32).reshape(D_MODEL, N_FOURIER_GROUPS, FOURIER_GROUP)
    w_p = jnp.einsum("mgc,gcd->mgd", w_in_g, wc, precision=_HI).reshape(D_MODEL, D_FNO)
    w_q = jnp.einsum("mgc,gcd->mgd", w_in_g, ws, precision=_HI).reshape(D_MODEL, D_FNO)
    return w_p, w_q


def kernel(x, norm1_g, w_in, ssm_a_re, ssm_a_im, ssm_log_step, ssm_b_re, ssm_b_im, ssm_c_re,
           ssm_c_im, ssm_d, w_glu, w_fourier, w_out, norm2_g, w_query, sub_keys, expert_u,
           expert_v, final_g):
    bsz, seq, d = x.shape
    n = bsz * seq
    n_chunks = seq // CHUNK
    ts = TM_IN // bsz
    assert d == D_MODEL and seq % (2 * TK_DFT) == 0 and TM_IN % bsz == 0 and n % TT_PEER == 0
    assert ts % CHUNK == 0 and ts % BF16_ROWS == 0 and seq % ts == 0
    assert norm1_g.shape[0] == 1, "single-layer block"
    layer = 0

    w_p, w_q = _fourier_weights(w_in[layer][:, D_SSM:], w_fourier[layer], seq)
    w_all = jnp.concatenate([w_in[layer][:, :D_SSM].astype(F32), w_p, w_q], axis=1).astype(BF16)
    wb, t_mat, wca, wcb, lam16 = _s5_weights(
        ssm_a_re[layer], ssm_a_im[layer], ssm_log_step[layer], ssm_b_re[layer], ssm_b_im[layer],
        ssm_c_re[layer], ssm_c_im[layer], ssm_d[layer])
    wq4 = w_query[layer].astype(F32).reshape(D_MODEL, PEER_HEADS, 2, PEER_HALF)
    wsc_t = jnp.einsum("dhpe,hpke->hpkd", wq4, sub_keys[layer].astype(F32), precision=_HI)
    wsc_t = wsc_t.reshape(PEER_HEADS * 2 * PEER_NKEYS, D_MODEL).astype(BF16)
    u_bf = _to_bf16(expert_u[layer], transpose=False)
    v_t = _to_bf16(expert_v[layer], transpose=True)

    u3, p, q = _inproj(x, norm1_g[layer].reshape(1, d).astype(F32), w_all, ts)

    yg = _s5(_to_groups(u3), wb, t_mat, wca, wcb, lam16, n_chunks, bsz)
    y3 = _from_groups(yg, n_chunks, bsz)

    f = jnp.concatenate(_seq_dft(p, q, bsz), axis=1)

    x1, h2t, sc_t = _merge(x, y3, f, w_glu[layer].astype(BF16), w_out[layer].astype(BF16),
                           norm2_g[layer].reshape(1, d).astype(F32), wsc_t, ts)

    al, ll, be, r2 = _gate_prep(sc_t)
    return _peer(h2t, x1, al, ll, be, r2, u_bf, v_t, final_g.reshape(1, d).astype(F32), ts)
```
